```python
import math
import jax, jax.numpy as jnp
from jax import lax
import numpy as np

D_MODEL = 2048
BATCH = 8
SEQ = 4096
DEPTH = 4

CONV_WIDTH = D_MODEL // 2
CONV_KERNEL = 31
SSM_WIDTH = D_MODEL // 2
SSM_GROUP = 16
SSM_GROUPS = SSM_WIDTH // SSM_GROUP
SSM_STATE = 64
D_FF = 4 * D_MODEL
IN_COLS = 2 * CONV_WIDTH + SSM_WIDTH + 2 * D_MODEL
RMS_EPS = 1e-6
LN_EPS = 1e-5

kernel_name = "hybrid_conformer_s5_gated_block"


def _rmsnorm(x, g):
    xf = x.astype(jnp.float32)
    xf = xf * lax.rsqrt(jnp.mean(xf * xf, axis=-1, keepdims=True) + RMS_EPS)
    return (xf * g.astype(jnp.float32)).astype(x.dtype)


def _layernorm(x, g, b):
    xf = x.astype(jnp.float32)
    mu = jnp.mean(xf, axis=-1, keepdims=True)
    xc = xf - mu
    var = jnp.mean(xc * xc, axis=-1, keepdims=True)
    y = xc * lax.rsqrt(var + LN_EPS) * g.astype(jnp.float32) + b.astype(jnp.float32)
    return y.astype(x.dtype)


def _conformer_conv(a, w_dw, b_dw, ln_g, ln_b):
    val, gate = jnp.split(a, 2, axis=-1)
    u = val * jax.nn.sigmoid(gate)
    u = lax.conv_general_dilated(
        u, w_dw[:, None, :].astype(u.dtype),
        window_strides=(1,),
        padding=((CONV_KERNEL - 1, 0),),
        dimension_numbers=("NWC", "WIO", "NWC"),
        feature_group_count=CONV_WIDTH) + b_dw
    u = _layernorm(u, ln_g, ln_b)
    return jax.nn.silu(u)


def _s5(u, a_re, a_im, log_dt, b_re, b_im, c_re, c_im, d_skip, w_glu):
    f32 = jnp.float32
    bsz, seq_len, _ = u.shape
    uf = u.astype(f32).reshape(bsz, seq_len, SSM_GROUPS, SSM_GROUP)
    lam = lax.complex(a_re.astype(f32), a_im.astype(f32))
    dt = jnp.exp(log_dt.astype(f32))[:, None]
    lam_bar = jnp.exp(lam * dt)
    b = lax.complex(b_re.astype(f32), b_im.astype(f32))
    b_bar = ((lam_bar - 1.0) / lam)[..., None] * b
    bu = lax.complex(jnp.einsum("blgh,gph->blgp", uf, b_bar.real),
                     jnp.einsum("blgh,gph->blgp", uf, b_bar.imag))
    lam_seq = jnp.broadcast_to(lam_bar, bu.shape)

    def combine(e1, e2):
        a1, s1 = e1
        a2, s2 = e2
        return a1 * a2, a2 * s1 + s2

    _, states = lax.associative_scan(combine, (lam_seq, bu), axis=1)
    c = lax.complex(c_re.astype(f32), c_im.astype(f32))
    y = jnp.einsum("blgp,ghp->blgh", states, c).real \
        + d_skip.astype(f32).reshape(SSM_GROUPS, SSM_GROUP) * uf
    y = jax.nn.gelu(y.reshape(bsz, seq_len, SSM_WIDTH)).astype(u.dtype)
    return y * jax.nn.sigmoid(y @ w_glu)


def setup_inputs(seed: int = 0) -> dict:
    key = jax.random.key(seed)
    ks = jax.random.split(key, 24)
    f32 = jnp.float32

    def nrm(k, shape, scale):
        return jax.random.normal(k, shape, f32) * scale

    L = DEPTH
    n = jnp.arange(SSM_STATE, dtype=f32)
    return {
        "x": nrm(ks[0], (BATCH, SEQ, D_MODEL), 1.0),
        "norm_mix": 1.0 + nrm(ks[1], (L, D_MODEL), 0.02),
        "w_in": nrm(ks[2], (L, D_MODEL, IN_COLS), D_MODEL ** -0.5),
        "w_dw": nrm(ks[3], (L, CONV_KERNEL, CONV_WIDTH), CONV_KERNEL ** -0.5),
        "b_dw": nrm(ks[4], (L, CONV_WIDTH), 0.02),
        "ln_g": 1.0 + nrm(ks[5], (L, CONV_WIDTH), 0.02),
        "ln_b": nrm(ks[6], (L, CONV_WIDTH), 0.02),
        "w_conv_out": nrm(ks[7], (L, CONV_WIDTH, D_MODEL), CONV_WIDTH ** -0.5),
        "a_re": -0.5 + nrm(ks[8], (L, SSM_GROUPS, SSM_STATE), 0.01),
        "a_im": math.pi * n + nrm(ks[9], (L, SSM_GROUPS, SSM_STATE), 0.01),
        "log_dt": jax.random.uniform(ks[10], (L, SSM_GROUPS), f32,
                                     math.log(1e-3), math.log(1e-1)),
        "b_re": nrm(ks[11], (L, SSM_GROUPS, SSM_STATE, SSM_GROUP), (2 * SSM_GROUP) ** -0.5),
        "b_im": nrm(ks[12], (L, SSM_GROUPS, SSM_STATE, SSM_GROUP), (2 * SSM_GROUP) ** -0.5),
        "c_re": nrm(ks[13], (L, SSM_GROUPS, SSM_GROUP, SSM_STATE), (2 * SSM_STATE) ** -0.5),
        "c_im": nrm(ks[14], (L, SSM_GROUPS, SSM_GROUP, SSM_STATE), (2 * SSM_STATE) ** -0.5),
        "d_skip": nrm(ks[15], (L, SSM_WIDTH), 1.0),
        "w_glu": nrm(ks[16], (L, SSM_WIDTH, SSM_WIDTH), SSM_WIDTH ** -0.5),
        "w_ssm_out": nrm(ks[17], (L, SSM_WIDTH, D_MODEL), SSM_WIDTH ** -0.5),
        "w_out": nrm(ks[18], (L, D_MODEL, D_MODEL), D_MODEL ** -0.5),
        "norm_mlp": 1.0 + nrm(ks[19], (L, D_MODEL), 0.02),
        "w_ff1": nrm(ks[20], (L, D_MODEL, D_FF), D_MODEL ** -0.5),
        "w_ff2": nrm(ks[21], (L, D_FF, D_MODEL), D_FF ** -0.5),
        "norm_final": 1.0 + nrm(ks[22], (D_MODEL,), 0.02),
    }


def reference(x, norm_mix, w_in, w_dw, b_dw, ln_g, ln_b, w_conv_out,
              a_re, a_im, log_dt, b_re, b_im, c_re, c_im, d_skip, w_glu,
              w_ssm_out, w_out, norm_mlp, w_ff1, w_ff2, norm_final):
    c0 = 2 * CONV_WIDTH
    c1 = c0 + SSM_WIDTH
    for l in range(DEPTH):
        h = _rmsnorm(x, norm_mix[l])
        proj = h @ w_in[l]
        a_conv = proj[..., :c0]
        u_ssm = proj[..., c0:c1]
        g_conv, g_ssm = jnp.split(jax.nn.sigmoid(proj[..., c1:]), 2, axis=-1)
        y_conv = _conformer_conv(a_conv, w_dw[l], b_dw[l], ln_g[l], ln_b[l]) @ w_conv_out[l]
        y_ssm = _s5(u_ssm, a_re[l], a_im[l], log_dt[l], b_re[l], b_im[l],
                    c_re[l], c_im[l], d_skip[l], w_glu[l]) @ w_ssm_out[l]
        x = x + (g_conv * y_conv + g_ssm * y_ssm) @ w_out[l]
        h = _rmsnorm(x, norm_mlp[l])
        x = x + jnp.square(jax.nn.relu(h @ w_ff1[l])) @ w_ff2[l]
    return _rmsnorm(x, norm_final)
```

```python
import functools

import jax
import jax.numpy as jnp
from jax import lax
from jax.experimental import pallas as pl
from jax.experimental.pallas import tpu as pltpu

RMS_EPS = 1e-6
LN_EPS = 1e-5

F32 = jnp.float32
BF16 = jnp.bfloat16

V7X_VMEM_BYTES = 64 * 1024 * 1024
V7X_LANES = 128
V7X_SUBLANES = 8
SSM_CHUNK = V7X_LANES


def _vmem_limit(block_bytes, scratch_bytes, temp_bytes):
    need = 2 * block_bytes + scratch_bytes + temp_bytes
    return int(min(V7X_VMEM_BYTES - 6 * 1024 * 1024, max(need, 16 * 1024 * 1024)))


def _params(semantics, limit):
    return pltpu.CompilerParams(dimension_semantics=semantics, vmem_limit_bytes=limit)


def _rmsnorm_rows(xf, g):
    ms = jnp.mean(xf * xf, axis=-1, keepdims=True)
    return xf * lax.rsqrt(ms + RMS_EPS) * g


def _ssm_prep_kernel(are_ref, aim_ref, ldt_ref, bre_ref, bim_ref,
                     lbr_ref, lbi_ref, bbr_ref, bbi_ref):
    ar = are_ref[...]
    ai = aim_ref[...]
    dt = jnp.exp(ldt_ref[...])
    mag = jnp.exp(ar * dt)
    lbr = mag * jnp.cos(ai * dt)
    lbi = mag * jnp.sin(ai * dt)
    lbr_ref[...] = lbr
    lbi_ref[...] = lbi
    nr = lbr - 1.0
    den = ar * ar + ai * ai
    fr = (nr * ar + lbi * ai) / den
    fi = (lbi * ar - nr * ai) / den
    br = bre_ref[...]
    bi = bim_ref[...]
    bbr_ref[...] = fr * br - fi * bi
    bbi_ref[...] = fr * bi + fi * br


def _ssm_prep(a_re, a_im, log_dt, b_re, b_im):
    nl, g, p = a_re.shape
    h = b_re.shape[-1]
    gp = g * p
    are = a_re.reshape(nl, 1, gp)
    aim = a_im.reshape(nl, 1, gp)
    ldt = jnp.broadcast_to(log_dt[:, :, None], (nl, g, p)).reshape(nl, 1, gp)
    bre = b_re.reshape(nl, gp, h).transpose(0, 2, 1)
    bim = b_im.reshape(nl, gp, h).transpose(0, 2, 1)
    row = pl.BlockSpec((None, 1, gp), lambda l: (l, 0, 0))
    mat = pl.BlockSpec((None, h, gp), lambda l: (l, 0, 0))
    return pl.pallas_call(
        _ssm_prep_kernel,
        grid=(nl,),
        in_specs=[row, row, row, mat, mat],
        out_specs=[row, row, mat, mat],
        out_shape=[jax.ShapeDtypeStruct((nl, 1, gp), F32)] * 2
        + [jax.ShapeDtypeStruct((nl, h, gp), F32)] * 2,
        name="ssm_prep",
    )(are, aim, ldt, bre, bim)


def _block_diag(w):
    nl, nc, gl, a, b = w.shape
    eye = jnp.eye(gl, dtype=w.dtype)
    return jnp.einsum("lcgab,gh->lcgahb", w, eye).reshape(nl, nc, gl * a, gl * b)


def _ssm_layouts(lbr, lbi, bbr, bbi, c_re, c_im, nb):
    nl, h, gp = bbr.shape
    g = c_re.shape[1]
    p = gp // g
    gl = SSM_CHUNK // h
    nc = g // gl
    def in_map(bb):
        bb = bb.reshape(nl, h, nc, gl, p).transpose(0, 2, 3, 1, 4)
        return _block_diag(bb)
    wb = jnp.concatenate([in_map(bbr), in_map(bbi)], axis=-1).astype(BF16)
    def out_map(c):
        c = c.reshape(nl, nc, gl, h, p).transpose(0, 1, 2, 4, 3)
        return _block_diag(c)
    wc = jnp.concatenate([out_map(c_re), out_map(-c_im)], axis=-2).astype(BF16)
    def lam_map(lb):
        return jnp.broadcast_to(lb.reshape(nl, nc, 1, gl * p), (nl, nc, nb, gl * p))
    return wb, wc, lam_map(lbr), lam_map(lbi)


def _inproj_kernel(x_ref, g_ref, w_ref, o_ref, h_ref, *, n_plain):
    j = pl.program_id(1)

    @pl.when(j == 0)
    def _():
        h_ref[...] = _rmsnorm_rows(x_ref[...], g_ref[...]).astype(BF16)

    acc = jnp.dot(h_ref[...], w_ref[...], preferred_element_type=F32)

    @pl.when(j < n_plain)
    def _():
        o_ref[...] = acc

    @pl.when(j >= n_plain)
    def _():
        o_ref[...] = jax.nn.sigmoid(acc)


def _inproj(x, g, w, *, gate_col, tm, tn):
    r, d = x.shape
    n = w.shape[1]
    blocks = tm * d * 4 + d * tn * 2 + tm * tn * 4
    return pl.pallas_call(
        functools.partial(_inproj_kernel, n_plain=gate_col // tn),
        grid=(r // tm, n // tn),
        in_specs=[pl.BlockSpec((tm, d), lambda i, j: (i, 0)),
                  pl.BlockSpec((1, d), lambda i, j: (0, 0)),
                  pl.BlockSpec((d, tn), lambda i, j: (0, j))],
        out_specs=pl.BlockSpec((tm, tn), lambda i, j: (i, j)),
        out_shape=jax.ShapeDtypeStruct((r, n), F32),
        scratch_shapes=[pltpu.VMEM((tm, d), BF16)],
        compiler_params=_params(("parallel", "arbitrary"),
                                _vmem_limit(blocks, tm * d * 2, tm * d * 4 + 2 * tm * tn * 4)),
        name="in_proj",
    )(x, g, w)


def _conv_kernel(val_ref, gate_ref, g0_ref, g1_ref, wdw_ref, bdw_ref, lng_ref, lnb_ref,
                 wout_ref, o_ref, ubuf, cbuf, *, tm, nb, ktaps, rb):
    i = pl.program_id(0)
    cw = val_ref.shape[1]
    halo = (ktaps - 1) * nb

    @pl.when(i == 0)
    def _():
        ubuf[0:halo, :] = jnp.zeros((halo, cw), F32)

    @pl.when(i > 0)
    def _():
        ubuf[0:halo, :] = ubuf[tm:tm + halo, :]

    ubuf[halo:halo + tm, :] = val_ref[...] * jax.nn.sigmoid(gate_ref[...])

    def row_block(b, carry):
        r0 = pl.multiple_of(b * rb, rb)
        for c in range(cw // V7X_LANES):
            cs = slice(c * V7X_LANES, (c + 1) * V7X_LANES)
            acc = jnp.broadcast_to(bdw_ref[:, cs], (rb, V7X_LANES))
            for k in range(ktaps):
                rows = pl.ds(pl.multiple_of(r0 + k * nb, nb), rb)
                acc = acc + wdw_ref[k:k + 1, cs] * ubuf[rows, cs]
            cbuf[pl.ds(r0, rb), cs] = acc
        return carry

    lax.fori_loop(0, tm // rb, row_block, 0)

    c = cbuf[...]
    mu = jnp.mean(c, axis=-1, keepdims=True)
    xc = c - mu
    var = jnp.mean(xc * xc, axis=-1, keepdims=True)
    y = xc * lax.rsqrt(var + LN_EPS) * lng_ref[...] + lnb_ref[...]
    s = (y * jax.nn.sigmoid(y)).astype(BF16)
    out = jnp.dot(s, wout_ref[...], preferred_element_type=F32)
    o_ref[:, 0:cw] = out[:, 0:cw] * g0_ref[...]
    o_ref[:, cw:2 * cw] = out[:, cw:2 * cw] * g1_ref[...]


def _conv_branch(proj, w_dw, b_dw, ln_g, ln_b, w_out, *, tm, nb, gate_blk):
    r = proj.shape[0]
    ktaps, cw = w_dw.shape
    d = w_out.shape[1]
    assert d == 2 * cw and tm >= (ktaps - 1) * nb
    rb = 8 * V7X_SUBLANES
    col = lambda j: pl.BlockSpec((tm, cw), lambda i: (i, j))
    full = lambda a: pl.BlockSpec(a.shape, lambda i: (0,) * a.ndim)
    halo = (ktaps - 1) * nb
    blocks = 4 * tm * cw * 4 + cw * d * 2 + tm * d * 4
    scratch = (halo + tm) * cw * 4 + tm * cw * 4
    return pl.pallas_call(
        functools.partial(_conv_kernel, tm=tm, nb=nb, ktaps=ktaps, rb=rb),
        grid=(r // tm,),
        in_specs=[col(0), col(1), col(gate_blk), col(gate_blk + 1),
                  full(w_dw), full(b_dw), full(ln_g), full(ln_b), full(w_out)],
        out_specs=pl.BlockSpec((tm, d), lambda i: (i, 0)),
        out_shape=jax.ShapeDtypeStruct((r, d), F32),
        scratch_shapes=[pltpu.VMEM((halo + tm, cw), F32), pltpu.VMEM((tm, cw), F32)],
        compiler_params=_params(("arbitrary",), _vmem_limit(blocks, scratch, 6 * tm * cw * 4)),
        name="conv_branch",
    )(proj, proj, proj, proj, w_dw, b_dw, ln_g, ln_b, w_out)


def _ssm_kernel(u_ref, g0_ref, g1_ref, mconv_ref, wb_ref, wc_ref, lamr_ref, lami_ref,
                dskip_ref, wglu_ref, wout_ref, o_ref, st_ref, bu_ref, y_ref, *, tm, nb):
    i = pl.program_id(0)
    sw = u_ref.shape[1]
    nc = sw // SSM_CHUNK
    half = bu_ref.shape[1] // 2
    re = slice(0, half)
    im = slice(half, 2 * half)

    @pl.when(i == 0)
    def _():
        st_ref[...] = jnp.zeros(st_ref.shape, F32)

    for c in range(nc):
        cs = slice(c * SSM_CHUNK, (c + 1) * SSM_CHUNK)
        uc = u_ref[:, cs]
        bu_ref[...] = jnp.dot(uc.astype(BF16), wb_ref[c], preferred_element_type=F32)
        lr = lamr_ref[c]
        li = lami_ref[c]

        def step(t, carry):
            sr, si = carry
            rows = pl.ds(pl.multiple_of(t * nb, nb), nb)
            nr = lr * sr - li * si + bu_ref[rows, re]
            ni = lr * si + li * sr + bu_ref[rows, im]
            bu_ref[rows, re] = nr
            bu_ref[rows, im] = ni
            return nr, ni

        sr, si = lax.fori_loop(0, tm // nb, step, (st_ref[c, :, re], st_ref[c, :, im]), unroll=8)
        st_ref[c, :, re] = sr
        st_ref[c, :, im] = si
        yc = jnp.dot(bu_ref[...].astype(BF16), wc_ref[c], preferred_element_type=F32)
        y_ref[:, cs] = yc + dskip_ref[:, cs] * uc

    y = jax.nn.gelu(y_ref[...])
    z = y * jax.nn.sigmoid(jnp.dot(y.astype(BF16), wglu_ref[...], preferred_element_type=F32))
    out = jnp.dot(z.astype(BF16), wout_ref[...], preferred_element_type=F32)
    o_ref[:, 0:sw] = (out[:, 0:sw] * g0_ref[...] + mconv_ref[:, 0:sw]).astype(BF16)
    o_ref[:, sw:2 * sw] = (out[:, sw:2 * sw] * g1_ref[...] + mconv_ref[:, sw:2 * sw]).astype(BF16)


def _ssm_branch(proj, mconv, wb, wc, lamr, lami, d_skip, w_glu, w_out, *, tm, nb, u_blk, gate_blk):
    r = proj.shape[0]
    sw = w_glu.shape[0]
    d = w_out.shape[1]
    nc, _, lanes = wb.shape
    assert d == 2 * sw
    col = lambda j: pl.BlockSpec((tm, sw), lambda i: (i, j))
    full = lambda a: pl.BlockSpec(a.shape, lambda i: (0,) * a.ndim)
    blocks = (3 * tm * sw * 4 + tm * d * 4 + wb.size * 2 + wc.size * 2 + 2 * lamr.size * 4
              + sw * sw * 2 + sw * d * 2 + tm * d * 2)
    scratch = nc * nb * lanes * 4 + tm * lanes * 4 + tm * sw * 4
    return pl.pallas_call(
        functools.partial(_ssm_kernel, tm=tm, nb=nb),
        grid=(r // tm,),
        in_specs=[col(u_blk), col(gate_blk), col(gate_blk + 1),
                  pl.BlockSpec((tm, d), lambda i: (i, 0)),
                  full(wb), full(wc), full(lamr), full(lami), full(d_skip), full(w_glu), full(w_out)],
        out_specs=pl.BlockSpec((tm, d), lambda i: (i, 0)),
        out_shape=jax.ShapeDtypeStruct((r, d), BF16),
        scratch_shapes=[pltpu.VMEM((nc, nb, lanes), F32), pltpu.VMEM((tm, lanes), F32),
                        pltpu.VMEM((tm, sw), F32)],
        compiler_params=_params(("arbitrary",), _vmem_limit(blocks, scratch, 8 * tm * sw * 4)),
        name="ssm_branch",
    )(proj, proj, proj, mconv, wb, wc, lamr, lami, d_skip, w_glu, w_out)


def _outproj_kernel(m_ref, w_ref, x_ref, o_ref):
    o_ref[...] = x_ref[...] + jnp.dot(m_ref[...], w_ref[...], preferred_element_type=F32)


def _outproj(merged, w, x, *, tm):
    r, d = x.shape
    blocks = tm * d * 2 + d * d * 2 + 2 * tm * d * 4
    return pl.pallas_call(
        _outproj_kernel,
        grid=(r // tm,),
        in_specs=[pl.BlockSpec((tm, d), lambda i: (i, 0)),
                  pl.BlockSpec((d, d), lambda i: (0, 0)),
                  pl.BlockSpec((tm, d), lambda i: (i, 0))],
        out_specs=pl.BlockSpec((tm, d), lambda i: (i, 0)),
        out_shape=jax.ShapeDtypeStruct((r, d), F32),
        compiler_params=_params(("parallel",), _vmem_limit(blocks, 0, tm * d * 4)),
        name="out_proj",
    )(merged, w, x)


def _mlp_kernel(x_ref, g_ref, w1_ref, w2_ref, gf_ref, o_ref, h_ref, *, final_norm):
    k = pl.program_id(1)

    @pl.when(k == 0)
    def _():
        xf = x_ref[...]
        h_ref[...] = _rmsnorm_rows(xf, g_ref[...]).astype(BF16)
        o_ref[...] = xf

    a = jnp.maximum(jnp.dot(h_ref[...], w1_ref[...], preferred_element_type=F32), 0.0)
    a = (a * a).astype(BF16)
    o_ref[...] += jnp.dot(a, w2_ref[...], preferred_element_type=F32)

    if final_norm:
        @pl.when(k == pl.num_programs(1) - 1)
        def _():
            o_ref[...] = _rmsnorm_rows(o_ref[...], gf_ref[...])


def _mlp(x, g, w1, w2, gf, *, tm, fk, final_norm):
    r, d = x.shape
    f = w1.shape[1]
    blocks = 2 * tm * d * 4 + 2 * d * fk * 2
    return pl.pallas_call(
        functools.partial(_mlp_kernel, final_norm=final_norm),
        grid=(r // tm, f // fk),
        in_specs=[pl.BlockSpec((tm, d), lambda i, k: (i, 0)),
                  pl.BlockSpec((1, d), lambda i, k: (0, 0)),
                  pl.BlockSpec((d, fk), lambda i, k: (0, k)),
                  pl.BlockSpec((fk, d), lambda i, k: (k, 0)),
                  pl.BlockSpec((1, d), lambda i, k: (0, 0))],
        out_specs=pl.BlockSpec((tm, d), lambda i, k: (i, 0)),
        out_shape=jax.ShapeDtypeStruct((r, d), F32),
        scratch_shapes=[pltpu.VMEM((tm, d), BF16)],
        compiler_params=_params(("parallel", "arbitrary"),
                                _vmem_limit(blocks, tm * d * 2, 2 * tm * fk * 4 + tm * d * 4)),
        name="mlp",
    )(x, g, w1, w2, gf)


def _tiles(r, nb, ktaps):
    tm_seq = min(r, 512)
    while tm_seq < (ktaps - 1) * nb:
        tm_seq *= 2
    tm_mm = min(r, 1024)
    assert r % tm_seq == 0 and r % tm_mm == 0 and tm_seq % nb == 0
    return tm_mm, tm_seq


def kernel(x, norm_mix, w_in, w_dw, b_dw, ln_g, ln_b, w_conv_out, a_re, a_im, log_dt, b_re, b_im,
           c_re, c_im, d_skip, w_glu, w_ssm_out, w_out, norm_mlp, w_ff1, w_ff2, norm_final):
    bsz, seq, d = x.shape
    depth = w_in.shape[0]
    ktaps, cw = w_dw.shape[1:]
    sw = w_glu.shape[1]
    assert bsz % V7X_SUBLANES == 0 and cw == sw and d == 2 * cw
    r = seq * bsz
    tm_mm, tm_seq = _tiles(r, bsz, ktaps)
    tn = cw
    u_blk = 2 * cw // tn
    gate_col = 2 * cw + sw

    lbr, lbi, bbr, bbi = _ssm_prep(a_re, a_im, log_dt, b_re, b_im)
    wb, wc, lamr, lami = _ssm_layouts(lbr, lbi, bbr, bbi, c_re, c_im, bsz)

    xt = x.transpose(1, 0, 2).reshape(r, d)
    row = lambda v: v.reshape(1, -1)
    for l in range(depth):
        proj = _inproj(xt, row(norm_mix[l]), w_in[l].astype(BF16), gate_col=gate_col, tm=tm_mm, tn=tn)
        mconv = _conv_branch(proj, w_dw[l], row(b_dw[l]), row(ln_g[l]), row(ln_b[l]),
                             w_conv_out[l].astype(BF16), tm=tm_seq, nb=bsz, gate_blk=gate_col // tn)
        merged = _ssm_branch(proj, mconv, wb[l], wc[l], lamr[l], lami[l], row(d_skip[l]),
                             w_glu[l].astype(BF16), w_ssm_out[l].astype(BF16),
                             tm=tm_seq, nb=bsz, u_blk=u_blk, gate_blk=gate_col // tn + d // tn)
        xt = _outproj(merged, w_out[l].astype(BF16), xt, tm=tm_mm)
        xt = _mlp(xt, row(norm_mlp[l]), w_ff1[l].astype(BF16), w_ff2[l].astype(BF16), row(norm_final),
                  tm=tm_mm, fk=512, final_norm=(l == depth - 1))
    return xt.reshape(seq, bsz, d).transpose(1, 0, 2)
```

```python
import functools

import jax
import jax.numpy as jnp
from jax import lax
from jax.experimental import pallas as pl
from jax.experimental.pallas import tpu as pltpu

RMS_EPS = 1e-6
LN_EPS = 1e-5

F32 = jnp.float32
BF16 = jnp.bfloat16

V7X_VMEM_BYTES = 64 * 1024 * 1024
V7X_LANES = 128
V7X_SUBLANES = 8
SSM_CHUNK = V7X_LANES
CONV_TILES = 8


def _vmem_limit(block_bytes, single_bytes, temp_bytes):
    need = 2 * block_bytes + single_bytes + temp_bytes
    return int(min(V7X_VMEM_BYTES - 6 * 1024 * 1024, max(need, 16 * 1024 * 1024)))


def _params(semantics, limit):
    return pltpu.CompilerParams(dimension_semantics=semantics, vmem_limit_bytes=limit)


def _layer_spec(a, l, single=False):
    mode = dict(pipeline_mode=pl.Buffered(1)) if single else {}
    return pl.BlockSpec((None,) + a.shape[1:], lambda *_: (l,) + (0,) * (a.ndim - 1), **mode)


def _rmsnorm_rows(xf, g):
    ms = jnp.mean(xf * xf, axis=-1, keepdims=True)
    return xf * lax.rsqrt(ms + RMS_EPS) * g


def _ssm_prep_kernel(are_ref, aim_ref, ldt_ref, bre_ref, bim_ref,
                     lbr_ref, lbi_ref, bbr_ref, bbi_ref):
    ar = are_ref[...]
    ai = aim_ref[...]
    dt = jnp.exp(ldt_ref[...])
    mag = jnp.exp(ar * dt)
    lbr = mag * jnp.cos(ai * dt)
    lbi = mag * jnp.sin(ai * dt)
    lbr_ref[...] = lbr
    lbi_ref[...] = lbi
    nr = lbr - 1.0
    den = ar * ar + ai * ai
    fr = (nr * ar + lbi * ai) / den
    fi = (lbi * ar - nr * ai) / den
    br = bre_ref[...]
    bi = bim_ref[...]
    bbr_ref[...] = fr * br - fi * bi
    bbi_ref[...] = fr * bi + fi * br


def _ssm_prep(a_re, a_im, log_dt, b_re, b_im):
    nl, g, p = a_re.shape
    h = b_re.shape[-1]
    gp = g * p
    are = a_re.reshape(nl, 1, gp)
    aim = a_im.reshape(nl, 1, gp)
    ldt = jnp.broadcast_to(log_dt[:, :, None], (nl, g, p)).reshape(nl, 1, gp)
    bre = b_re.reshape(nl, gp, h).transpose(0, 2, 1)
    bim = b_im.reshape(nl, gp, h).transpose(0, 2, 1)
    row = pl.BlockSpec((None, 1, gp), lambda l: (l, 0, 0))
    mat = pl.BlockSpec((None, h, gp), lambda l: (l, 0, 0))
    return pl.pallas_call(
        _ssm_prep_kernel,
        grid=(nl,),
        in_specs=[row, row, row, mat, mat],
        out_specs=[row, row, mat, mat],
        out_shape=[jax.ShapeDtypeStruct((nl, 1, gp), F32)] * 2
        + [jax.ShapeDtypeStruct((nl, h, gp), F32)] * 2,
        name="ssm_prep",
    )(are, aim, ldt, bre, bim)


def _block_diag(w):
    nl, nc, gl, a, b = w.shape
    eye = jnp.eye(gl, dtype=w.dtype)
    return jnp.einsum("lcgab,gh->lcgahb", w, eye).reshape(nl, nc, gl * a, gl * b)


def _ssm_layouts(lbr, lbi, bbr, bbi, c_re, c_im, nb):
    nl, h, gp = bbr.shape
    g = c_re.shape[1]
    p = gp // g
    gl = SSM_CHUNK // h
    nc = g // gl
    def in_map(bb):
        bb = bb.reshape(nl, h, nc, gl, p).transpose(0, 2, 3, 1, 4)
        return _block_diag(bb)
    wb = jnp.concatenate([in_map(bbr), in_map(bbi)], axis=-1).astype(BF16)
    def out_map(c):
        c = c.reshape(nl, nc, gl, h, p).transpose(0, 1, 2, 4, 3)
        return _block_diag(c)
    wc = jnp.concatenate([out_map(c_re), out_map(-c_im)], axis=-2).astype(BF16)
    def lam_map(lb):
        return jnp.broadcast_to(lb.reshape(nl, nc, 1, gl * p), (nl, nc, nb, gl * p))
    return wb, wc, lam_map(lbr), lam_map(lbi)


def _inproj_kernel(x_ref, g_ref, w_ref, o_ref, h_ref, *, n_plain):
    j = pl.program_id(1)

    @pl.when(j == 0)
    def _():
        h_ref[...] = _rmsnorm_rows(x_ref[...], g_ref[...]).astype(BF16)

    acc = jnp.dot(h_ref[...], w_ref[...], preferred_element_type=F32)
    o_ref[...] = jnp.where(j >= n_plain, jax.nn.sigmoid(acc), acc)


def _inproj(x, g, w, l, *, gate_col, tm, tn):
    r, d = x.shape
    n = w.shape[2]
    blocks = tm * d * 4 + d * tn * 2 + tm * tn * 4
    return pl.pallas_call(
        functools.partial(_inproj_kernel, n_plain=gate_col // tn),
        grid=(r // tm, n // tn),
        in_specs=[pl.BlockSpec((tm, d), lambda i, j: (i, 0)),
                  _layer_spec(g, l),
                  pl.BlockSpec((None, d, tn), lambda i, j: (l, 0, j))],
        out_specs=pl.BlockSpec((tm, tn), lambda i, j: (i, j)),
        out_shape=jax.ShapeDtypeStruct((r, n), F32),
        scratch_shapes=[pltpu.VMEM((tm, d), BF16)],
        compiler_params=_params(("parallel", "arbitrary"),
                                _vmem_limit(blocks, tm * d * 2, tm * d * 4 + 2 * tm * tn * 4)),
        name="in_proj",
    )(x, g, w)


def _conv_module(val_ref, gate_ref, wdw_ref, bdw_ref, lng_ref, lnb_ref, ubuf, cbuf, *, tm, nb, ktaps):
    cw = val_ref.shape[1]
    halo = (ktaps - 1) * nb
    ubuf[halo:halo + tm, :] = val_ref[...] * jax.nn.sigmoid(gate_ref[...])
    rb = CONV_TILES * nb
    for b in range(tm // rb):
        for c in range(cw // V7X_LANES):
            cs = slice(c * V7X_LANES, (c + 1) * V7X_LANES)
            r0 = b * rb
            tiles = [ubuf[r0 + m * nb:r0 + (m + 1) * nb, cs] for m in range(CONV_TILES + ktaps - 1)]
            taps = [wdw_ref[k:k + 1, cs] for k in range(ktaps)]
            bias = jnp.broadcast_to(bdw_ref[:, cs], (nb, V7X_LANES))
            for j in range(CONV_TILES):
                acc = bias
                for k in range(ktaps):
                    acc = acc + taps[k] * tiles[j + k]
                cbuf[r0 + j * nb:r0 + (j + 1) * nb, cs] = acc
    ubuf[0:halo, :] = ubuf[tm:tm + halo, :]
    c = cbuf[...]
    mu = jnp.mean(c, axis=-1, keepdims=True)
    xc = c - mu
    var = jnp.mean(xc * xc, axis=-1, keepdims=True)
    y = xc * lax.rsqrt(var + LN_EPS) * lng_ref[...] + lnb_ref[...]
    return y * jax.nn.sigmoid(y)


def _s5_module(u_ref, wb_ref, wc_ref, lamr_ref, lami_ref, dskip_ref, st_ref, *, tm, nb):
    sw = u_ref.shape[1]
    half = st_ref.shape[2] // 2
    ys = []
    for c in range(sw // SSM_CHUNK):
        cs = slice(c * SSM_CHUNK, (c + 1) * SSM_CHUNK)
        uc = u_ref[:, cs]
        bu = jnp.dot(uc.astype(BF16), wb_ref[c], preferred_element_type=F32)
        lr = lamr_ref[c]
        li = lami_ref[c]
        sr = st_ref[c, :, 0:half]
        si = st_ref[c, :, half:2 * half]
        srs, sis = [], []
        for t in range(tm // nb):
            rows = slice(t * nb, (t + 1) * nb)
            sr, si = (lr * sr - li * si + bu[rows, 0:half],
                      lr * si + li * sr + bu[rows, half:2 * half])
            srs.append(sr)
            sis.append(si)
        st_ref[c, :, 0:half] = sr
        st_ref[c, :, half:2 * half] = si
        states = jnp.concatenate([jnp.concatenate(srs, axis=0), jnp.concatenate(sis, axis=0)], axis=1)
        yc = jnp.dot(states.astype(BF16), wc_ref[c], preferred_element_type=F32)
        ys.append(yc + dskip_ref[:, cs] * uc)
    return jax.nn.gelu(jnp.concatenate(ys, axis=1))


def _mixer_kernel(val_ref, gate_ref, u_ref, gc0_ref, gc1_ref, gs0_ref, gs1_ref, x_ref,
                  wdw_ref, bdw_ref, lng_ref, lnb_ref, wcout_ref,
                  wb_ref, wc_ref, lamr_ref, lami_ref, dskip_ref, wglu_ref, wsout_ref, wout_ref,
                  o_ref, ubuf, cbuf, st_ref, *, tm, nb, ktaps):
    cw = val_ref.shape[1]
    sw = u_ref.shape[1]

    @pl.when(pl.program_id(0) == 0)
    def _():
        ubuf[0:(ktaps - 1) * nb, :] = jnp.zeros(((ktaps - 1) * nb, cw), F32)
        st_ref[...] = jnp.zeros(st_ref.shape, F32)

    y = _s5_module(u_ref, wb_ref, wc_ref, lamr_ref, lami_ref, dskip_ref, st_ref, tm=tm, nb=nb)
    z = y * jax.nn.sigmoid(jnp.dot(y.astype(BF16), wglu_ref[...], preferred_element_type=F32))
    y_ssm = jnp.dot(z.astype(BF16), wsout_ref[...], preferred_element_type=F32)

    s = _conv_module(val_ref, gate_ref, wdw_ref, bdw_ref, lng_ref, lnb_ref, ubuf, cbuf,
                     tm=tm, nb=nb, ktaps=ktaps)
    y_conv = jnp.dot(s.astype(BF16), wcout_ref[...], preferred_element_type=F32)

    g_conv = jnp.concatenate([gc0_ref[...], gc1_ref[...]], axis=1)
    g_ssm = jnp.concatenate([gs0_ref[...], gs1_ref[...]], axis=1)
    merged = (g_conv * y_conv + g_ssm * y_ssm).astype(BF16)
    o_ref[...] = x_ref[...] + jnp.dot(merged, wout_ref[...], preferred_element_type=F32)


def _mixer(proj, x, conv_p, ssm_p, w_out, l, *, tm, nb):
    r, d = x.shape
    w_dw, b_dw, ln_g, ln_b, w_cout = conv_p
    wb, wc, lamr, lami, d_skip, w_glu, w_sout = ssm_p
    ktaps, cw = w_dw.shape[1:]
    sw = w_glu.shape[1]
    nc, _, lanes = wb.shape[1:]
    halo = (ktaps - 1) * nb
    assert cw == sw and d == 2 * cw and tm >= halo and tm % (CONV_TILES * nb) == 0
    col = lambda j: pl.BlockSpec((tm, cw), lambda i: (i, j))
    row = pl.BlockSpec((tm, d), lambda i: (i, 0))
    params = (w_dw, b_dw, ln_g, ln_b, w_cout, wb, wc, lamr, lami, d_skip, w_glu, w_sout, w_out)
    single = sum(a[0].size * a.dtype.itemsize for a in params)
    scratch = (halo + tm) * cw * 4 + tm * cw * 4 + nc * nb * lanes * 4
    blocks = 7 * tm * cw * 4 + 2 * tm * d * 4
    return pl.pallas_call(
        functools.partial(_mixer_kernel, tm=tm, nb=nb, ktaps=ktaps),
        grid=(r // tm,),
        in_specs=[col(j) for j in range(7)] + [row] + [_layer_spec(a, l, single=True) for a in params],
        out_specs=row,
        out_shape=jax.ShapeDtypeStruct((r, d), F32),
        scratch_shapes=[pltpu.VMEM((halo + tm, cw), F32), pltpu.VMEM((tm, cw), F32),
                        pltpu.VMEM((nc, nb, lanes), F32)],
        compiler_params=_params(("arbitrary",), _vmem_limit(blocks, single + scratch, 10 * tm * cw * 4)),
        name="mixer",
    )(*([proj] * 7), x, *params)


def _mlp_kernel(x_ref, g_ref, w1_ref, w2_ref, gf_ref, o_ref, h_ref, *, final_norm):
    k = pl.program_id(1)

    @pl.when(k == 0)
    def _():
        xf = x_ref[...]
        h_ref[...] = _rmsnorm_rows(xf, g_ref[...]).astype(BF16)
        o_ref[...] = xf

    a = jnp.maximum(jnp.dot(h_ref[...], w1_ref[...], preferred_element_type=F32), 0.0)
    a = (a * a).astype(BF16)
    o_ref[...] += jnp.dot(a, w2_ref[...], preferred_element_type=F32)

    if final_norm:
        @pl.when(k == pl.num_programs(1) - 1)
        def _():
            o_ref[...] = _rmsnorm_rows(o_ref[...], gf_ref[...])


def _mlp(x, g, w1, w2, gf, l, *, tm, fk, final_norm):
    r, d = x.shape
    f = w1.shape[2]
    blocks = 2 * tm * d * 4 + 2 * d * fk * 2
    return pl.pallas_call(
        functools.partial(_mlp_kernel, final_norm=final_norm),
        grid=(r // tm, f // fk),
        in_specs=[pl.BlockSpec((tm, d), lambda i, k: (i, 0)),
                  _layer_spec(g, l),
                  pl.BlockSpec((None, d, fk), lambda i, k: (l, 0, k)),
                  pl.BlockSpec((None, fk, d), lambda i, k: (l, k, 0)),
                  pl.BlockSpec((1, d), lambda i, k: (0, 0))],
        out_specs=pl.BlockSpec((tm, d), lambda i, k: (i, 0)),
        out_shape=jax.ShapeDtypeStruct((r, d), F32),
        scratch_shapes=[pltpu.VMEM((tm, d), BF16)],
        compiler_params=_params(("parallel", "arbitrary"),
                                _vmem_limit(blocks, tm * d * 2, 2 * tm * fk * 4 + tm * d * 4)),
        name="mlp",
    )(x, g, w1, w2, gf)


def _tiles(r, nb, ktaps):
    tm_seq = min(r, 256)
    while tm_seq < (ktaps - 1) * nb:
        tm_seq *= 2
    tm_mm = min(r, 1024)
    assert r % tm_seq == 0 and r % tm_mm == 0 and tm_seq % nb == 0
    return tm_mm, tm_seq


def kernel(x, norm_mix, w_in, w_dw, b_dw, ln_g, ln_b, w_conv_out, a_re, a_im, log_dt, b_re, b_im,
           c_re, c_im, d_skip, w_glu, w_ssm_out, w_out, norm_mlp, w_ff1, w_ff2, norm_final):
    bsz, seq, d = x.shape
    depth = w_in.shape[0]
    ktaps, cw = w_dw.shape[1:]
    sw = w_glu.shape[1]
    assert bsz % V7X_SUBLANES == 0 and cw == sw and d == 2 * cw
    r = seq * bsz
    tm_mm, tm_seq = _tiles(r, bsz, ktaps)
    gate_col = 2 * cw + sw

    lbr, lbi, bbr, bbi = _ssm_prep(a_re, a_im, log_dt, b_re, b_im)
    wb, wc, lamr, lami = _ssm_layouts(lbr, lbi, bbr, bbi, c_re, c_im, bsz)
    rows = lambda v: v[:, None, :]
    conv_p = (w_dw, rows(b_dw), rows(ln_g), rows(ln_b), w_conv_out.astype(BF16))
    ssm_p = (wb, wc, lamr, lami, rows(d_skip), w_glu.astype(BF16), w_ssm_out.astype(BF16))
    w_in_b, w_out_b = w_in.astype(BF16), w_out.astype(BF16)
    w_ff1_b, w_ff2_b = w_ff1.astype(BF16), w_ff2.astype(BF16)
    g_mix, g_mlp, g_fin = rows(norm_mix), rows(norm_mlp), norm_final.reshape(1, d)

    xt = x.transpose(1, 0, 2).reshape(r, d)
    for l in range(depth):
        proj = _inproj(xt, g_mix, w_in_b, l, gate_col=gate_col, tm=tm_mm, tn=cw)
        xt = _mixer(proj, xt, conv_p, ssm_p, w_out_b, l, tm=tm_seq, nb=bsz)
        xt = _mlp(xt, g_mlp, w_ff1_b, w_ff2_b, g_fin, l, tm=tm_mm, fk=512, final_norm=(l == depth - 1))
    return xt.reshape(seq, bsz, d).transpose(1, 0, 2)
```

```python
import functools

import jax
import jax.numpy as jnp
from jax import lax
from jax.experimental import pallas as pl
from jax.experimental.pallas import tpu as pltpu

RMS_EPS = 1e-6
LN_EPS = 1e-5

F32 = jnp.float32
BF16 = jnp.bfloat16

V7X_VMEM_BYTES = 64 * 1024 * 1024
V7X_LANES = 128
V7X_SUBLANES = 8
SSM_CHUNK = V7X_LANES
CONV_TILES = 8


def _vmem_limit(block_bytes, single_bytes, temp_bytes):
    need = 2 * block_bytes + single_bytes + temp_bytes
    return int(min(V7X_VMEM_BYTES - 6 * 1024 * 1024, max(need, 16 * 1024 * 1024)))


def _params(semantics, limit):
    return pltpu.CompilerParams(dimension_semantics=semantics, vmem_limit_bytes=limit)


def _layer_spec(a, l, single=False):
    mode = dict(pipeline_mode=pl.Buffered(1)) if single else {}
    return pl.BlockSpec((None,) + a.shape[1:], lambda *_: (l,) + (0,) * (a.ndim - 1), **mode)


def _rmsnorm_rows(xf, g):
    ms = jnp.mean(xf * xf, axis=-1, keepdims=True)
    return xf * lax.rsqrt(ms + RMS_EPS) * g


def _ssm_prep_kernel(are_ref, aim_ref, ldt_ref, bre_ref, bim_ref,
                     lbr_ref, lbi_ref, bbr_ref, bbi_ref):
    ar = are_ref[...]
    ai = aim_ref[...]
    dt = jnp.exp(ldt_ref[...])
    mag = jnp.exp(ar * dt)
    lbr = mag * jnp.cos(ai * dt)
    lbi = mag * jnp.sin(ai * dt)
    lbr_ref[...] = lbr
    lbi_ref[...] = lbi
    nr = lbr - 1.0
    den = ar * ar + ai * ai
    fr = (nr * ar + lbi * ai) / den
    fi = (lbi * ar - nr * ai) / den
    br = bre_ref[...]
    bi = bim_ref[...]
    bbr_ref[...] = fr * br - fi * bi
    bbi_ref[...] = fr * bi + fi * br


def _ssm_prep(a_re, a_im, log_dt, b_re, b_im):
    nl, g, p = a_re.shape
    h = b_re.shape[-1]
    gp = g * p
    are = a_re.reshape(nl, 1, gp)
    aim = a_im.reshape(nl, 1, gp)
    ldt = jnp.broadcast_to(log_dt[:, :, None], (nl, g, p)).reshape(nl, 1, gp)
    bre = b_re.reshape(nl, gp, h).transpose(0, 2, 1)
    bim = b_im.reshape(nl, gp, h).transpose(0, 2, 1)
    row = pl.BlockSpec((None, 1, gp), lambda l: (l, 0, 0))
    mat = pl.BlockSpec((None, h, gp), lambda l: (l, 0, 0))
    return pl.pallas_call(
        _ssm_prep_kernel,
        grid=(nl,),
        in_specs=[row, row, row, mat, mat],
        out_specs=[row, row, mat, mat],
        out_shape=[jax.ShapeDtypeStruct((nl, 1, gp), F32)] * 2
        + [jax.ShapeDtypeStruct((nl, h, gp), F32)] * 2,
        name="ssm_prep",
    )(are, aim, ldt, bre, bim)


def _block_diag(w):
    nl, nc, gl, a, b = w.shape
    eye = jnp.eye(gl, dtype=w.dtype)
    return jnp.einsum("lcgab,gh->lcgahb", w, eye).reshape(nl, nc, gl * a, gl * b)


def _ssm_layouts(lbr, lbi, bbr, bbi, c_re, c_im, nb):
    nl, h, gp = bbr.shape
    g = c_re.shape[1]
    p = gp // g
    gl = SSM_CHUNK // h
    nc = g // gl
    def in_map(bb):
        bb = bb.reshape(nl, h, nc, gl, p).transpose(0, 2, 3, 1, 4)
        return _block_diag(bb)
    wb = jnp.concatenate([in_map(bbr), in_map(bbi)], axis=-1).astype(BF16)
    def out_map(c):
        c = c.reshape(nl, nc, gl, h, p).transpose(0, 1, 2, 4, 3)
        return _block_diag(c)
    wc = jnp.concatenate([out_map(c_re), out_map(-c_im)], axis=-2).astype(BF16)
    def lam_map(lb):
        return jnp.broadcast_to(lb.reshape(nl, nc, 1, gl * p), (nl, nc, nb, gl * p))
    return wb, wc, lam_map(lbr), lam_map(lbi)


def _inproj_kernel(x_ref, g_ref, w_ref, o_ref, h_ref, *, n_plain):
    j = pl.program_id(1)

    @pl.when(j == 0)
    def _():
        h_ref[...] = _rmsnorm_rows(x_ref[...], g_ref[...]).astype(BF16)

    acc = jnp.dot(h_ref[...], w_ref[...], preferred_element_type=F32)
    o_ref[...] = jnp.where(j >= n_plain, jax.nn.sigmoid(acc), acc)


def _inproj(x, g, w, l, *, gate_col, tm, tn):
    r, d = x.shape
    n = w.shape[2]
    blocks = tm * d * 4 + d * tn * 2 + tm * tn * 4
    return pl.pallas_call(
        functools.partial(_inproj_kernel, n_plain=gate_col // tn),
        grid=(r // tm, n // tn),
        in_specs=[pl.BlockSpec((tm, d), lambda i, j: (i, 0)),
                  _layer_spec(g, l),
                  pl.BlockSpec((None, d, tn), lambda i, j: (l, 0, j))],
        out_specs=pl.BlockSpec((tm, tn), lambda i, j: (i, j)),
        out_shape=jax.ShapeDtypeStruct((r, n), F32),
        scratch_shapes=[pltpu.VMEM((tm, d), BF16)],
        compiler_params=_params(("parallel", "arbitrary"),
                                _vmem_limit(blocks, tm * d * 2, tm * d * 4 + 2 * tm * tn * 4)),
        name="in_proj",
    )(x, g, w)


def _interleave(streams, totals):
    done = [0.0] * len(streams)
    live = [True] * len(streams)
    while any(live):
        k = min((s for s in range(len(streams)) if live[s]), key=lambda s: done[s] / totals[s])
        try:
            done[k] += next(streams[k])
        except StopIteration:
            live[k] = False


_COST = dict(glu=100, conv=130, ln=400, bu=256, scan=256, cy=350, sglu=300, sso=256, cvo=300, out=512)


def _cur_stream(val_ref, gate_ref, u_ref, wdw_ref, bdw_ref, lng_ref, lnb_ref, wb_ref, wc_ref, lamr_ref,
                lami_ref, dskip_ref, ubuf, cbuf, st_ref, y_w, y16_w, s16_w, *, tm, nb, ktaps):
    cw = val_ref.shape[1]
    halo = (ktaps - 1) * nb
    rb = CONV_TILES * nb
    half = st_ref.shape[2] // 2
    n_chunks = u_ref.shape[1] // SSM_CHUNK
    blocks = [(b, c) for b in range(tm // rb) for c in range(cw // V7X_LANES)]
    per_chunk = -(-len(blocks) // n_chunks)
    glu_done = set()

    def glu_piece(b):
        rows = slice(b * rb, (b + 1) * rb)
        ubuf[halo + b * rb:halo + (b + 1) * rb, :] = val_ref[rows, :] * jax.nn.sigmoid(gate_ref[rows, :])

    def conv_block(b, c):
        cs = slice(c * V7X_LANES, (c + 1) * V7X_LANES)
        r0 = pl.multiple_of(jnp.minimum(pl.program_id(0), 0) * rb + b * rb, rb)
        tiles = {}

        def tile(m):
            if m not in tiles:
                tiles[m] = ubuf[pl.ds(r0 + m * nb, nb), cs]
            return tiles[m]

        acc = [jnp.broadcast_to(bdw_ref[:, cs], (nb, V7X_LANES))] * CONV_TILES
        for k in range(ktaps):
            wk = jnp.broadcast_to(wdw_ref[k:k + 1, cs], (nb, V7X_LANES))
            for j in range(CONV_TILES):
                acc[j] = acc[j] + wk * tile(j + k)
        for j in range(CONV_TILES):
            cbuf[pl.ds(r0 + j * nb, nb), cs] = acc[j]

    def norm_piece(b):
        rows = slice(b * rb, (b + 1) * rb)
        c = cbuf[rows, :]
        mu = jnp.mean(c, axis=-1, keepdims=True)
        xc = c - mu
        var = jnp.mean(xc * xc, axis=-1, keepdims=True)
        v = xc * lax.rsqrt(var + LN_EPS) * lng_ref[...] + lnb_ref[...]
        s16_w[rows, :] = (v * jax.nn.sigmoid(v)).astype(BF16)

    def conv_items(blks):
        for b, c in blks:
            if b not in glu_done:
                glu_done.add(b)
                glu_piece(b)
                yield _COST["glu"]
            conv_block(b, c)
            yield _COST["conv"]
            if c == cw // V7X_LANES - 1:
                norm_piece(b)
                yield _COST["ln"]

    for c in range(n_chunks):
        mine = blocks[c * per_chunk:(c + 1) * per_chunk]
        cs = slice(c * SSM_CHUNK, (c + 1) * SSM_CHUNK)
        uc = u_ref[:, cs]
        bu = jnp.dot(uc.astype(BF16), wb_ref[c], preferred_element_type=F32)
        yield _COST["bu"]
        yield from conv_items(mine[:len(mine) // 2])
        lr = lamr_ref[c]
        li = lami_ref[c]
        sr = st_ref[c, :, 0:half]
        si = st_ref[c, :, half:2 * half]
        srs, sis = [], []
        for t in range(tm // nb):
            rows = slice(t * nb, (t + 1) * nb)
            sr, si = (lr * sr - li * si + bu[rows, 0:half],
                      lr * si + li * sr + bu[rows, half:2 * half])
            srs.append(sr)
            sis.append(si)
        st_ref[c, :, 0:half] = sr
        st_ref[c, :, half:2 * half] = si
        states = jnp.concatenate([jnp.concatenate(srs, axis=0), jnp.concatenate(sis, axis=0)], axis=1)
        yield _COST["scan"]
        yield from conv_items(mine[len(mine) // 2:])
        yc = jnp.dot(states.astype(BF16), wc_ref[c], preferred_element_type=F32)
        y = jax.nn.gelu(yc + dskip_ref[:, cs] * uc)
        y_w[:, cs] = y
        y16_w[:, cs] = y.astype(BF16)
        yield _COST["cy"]
    ubuf[0:halo, :] = ubuf[tm:tm + halo, :]


def _prev_stream(gc_refs, gs_refs, x_ref, wcout_ref, wglu_ref, wsout_ref, wout_ref,
                 o_ref, y_r, y16_r, s16_r, *, piece_cols):
    sw = wglu_ref.shape[0]
    d = wout_ref.shape[1]
    col = lambda n: slice(n * piece_cols, (n + 1) * piece_cols)
    y16 = y16_r[...]
    s16 = s16_r[...]

    zs = []
    for n in range(sw // piece_cols):
        t = jnp.dot(y16, wglu_ref[:, col(n)], preferred_element_type=F32)
        zs.append((y_r[:, col(n)] * jax.nn.sigmoid(t)).astype(BF16))
        yield _COST["sglu"]
    z16 = jnp.concatenate(zs, axis=1)

    def gate(refs, n):
        per = refs[0].shape[1] // piece_cols
        return refs[n // per][:, col(n % per)]

    ms = []
    for n in range(d // piece_cols):
        y_ssm = jnp.dot(z16, wsout_ref[:, col(n)], preferred_element_type=F32)
        yield _COST["sso"]
        y_conv = jnp.dot(s16, wcout_ref[:, col(n)], preferred_element_type=F32)
        ms.append((gate(gc_refs, n) * y_conv + gate(gs_refs, n) * y_ssm).astype(BF16))
        yield _COST["cvo"]
    m16 = jnp.concatenate(ms, axis=1)

    for n in range(d // piece_cols):
        o_ref[:, col(n)] = x_ref[:, col(n)] + jnp.dot(m16, wout_ref[:, col(n)], preferred_element_type=F32)
        yield _COST["out"]


def _mixer_kernel(val_ref, gate_ref, u_ref, gc0_ref, gc1_ref, gs0_ref, gs1_ref, x_ref,
                  wdw_ref, bdw_ref, lng_ref, lnb_ref, wcout_ref,
                  wb_ref, wc_ref, lamr_ref, lami_ref, dskip_ref, wglu_ref, wsout_ref, wout_ref,
                  o_ref, ubuf, cbuf, st_ref, ybuf, y16buf, s16buf, *, tm, nb, ktaps):
    i = pl.program_id(0)
    cw = val_ref.shape[1]
    slot = lax.rem(i, 2)

    @pl.when(i == 0)
    def _():
        ubuf[0:(ktaps - 1) * nb, :] = jnp.zeros(((ktaps - 1) * nb, cw), F32)
        st_ref[...] = jnp.zeros(st_ref.shape, F32)
        ybuf[1] = jnp.zeros(ybuf.shape[1:], F32)
        y16buf[1] = jnp.zeros(y16buf.shape[1:], BF16)
        s16buf[1] = jnp.zeros(s16buf.shape[1:], BF16)

    cur = _cur_stream(val_ref, gate_ref, u_ref, wdw_ref, bdw_ref, lng_ref, lnb_ref, wb_ref, wc_ref,
                      lamr_ref, lami_ref, dskip_ref, ubuf, cbuf, st_ref,
                      ybuf.at[slot], y16buf.at[slot], s16buf.at[slot], tm=tm, nb=nb, ktaps=ktaps)
    prev = _prev_stream((gc0_ref, gc1_ref), (gs0_ref, gs1_ref), x_ref, wcout_ref, wglu_ref, wsout_ref,
                        wout_ref, o_ref, ybuf.at[1 - slot], y16buf.at[1 - slot], s16buf.at[1 - slot],
                        piece_cols=2 * V7X_LANES)
    n_blocks = (tm // (CONV_TILES * nb)) * (cw // V7X_LANES)
    n_chunks = u_ref.shape[1] // SSM_CHUNK
    n_cols = wout_ref.shape[1] // (2 * V7X_LANES)
    total_cur = (tm // (CONV_TILES * nb)) * (_COST["glu"] + _COST["ln"]) + n_blocks * _COST["conv"] + (
        n_chunks * (_COST["bu"] + _COST["scan"] + _COST["cy"]))
    total_prev = (n_cols // 2) * _COST["sglu"] + n_cols * (_COST["sso"] + _COST["cvo"] + _COST["out"])
    _interleave([cur, prev], [total_cur, total_prev])


def _mixer(proj, x, conv_p, ssm_p, w_out, l, *, tm, nb):
    r, d = x.shape
    w_dw, b_dw, ln_g, ln_b, w_cout = conv_p
    wb, wc, lamr, lami, d_skip, w_glu, w_sout = ssm_p
    ktaps, cw = w_dw.shape[1:]
    sw = w_glu.shape[1]
    nc, _, lanes = wb.shape[1:]
    halo = (ktaps - 1) * nb
    n = r // tm
    assert cw == sw and d == 2 * cw and tm >= halo and tm % (CONV_TILES * nb) == 0
    cur = lambda j: pl.BlockSpec((tm, cw), lambda i: (jnp.minimum(i, n - 1), j))
    prev = lambda j: pl.BlockSpec((tm, cw), lambda i: (jnp.maximum(i - 1, 0), j))
    prev_row = pl.BlockSpec((tm, d), lambda i: (jnp.maximum(i - 1, 0), 0))
    params = (w_dw, b_dw, ln_g, ln_b, w_cout, wb, wc, lamr, lami, d_skip, w_glu, w_sout, w_out)
    single = sum(a[0].size * a.dtype.itemsize for a in params)
    scratch = (halo + tm) * cw * 4 + tm * cw * 4 + nc * nb * lanes * 4 + 2 * tm * (sw * 6 + cw * 2)
    blocks = 7 * tm * cw * 4 + 2 * tm * d * 4
    return pl.pallas_call(
        functools.partial(_mixer_kernel, tm=tm, nb=nb, ktaps=ktaps),
        grid=(n + 1,),
        in_specs=[cur(0), cur(1), cur(2), prev(3), prev(4), prev(5), prev(6), prev_row]
        + [_layer_spec(a, l, single=True) for a in params],
        out_specs=prev_row,
        out_shape=jax.ShapeDtypeStruct((r, d), F32),
        scratch_shapes=[pltpu.VMEM((halo + tm, cw), F32), pltpu.VMEM((tm, cw), F32),
                        pltpu.VMEM((nc, nb, lanes), F32), pltpu.VMEM((2, tm, sw), F32),
                        pltpu.VMEM((2, tm, sw), BF16), pltpu.VMEM((2, tm, cw), BF16)],
        compiler_params=_params(("arbitrary",), _vmem_limit(blocks, single + scratch, 10 * tm * cw * 4)),
        name="mixer",
    )(*([proj] * 7), x, *params)


def _mlp_kernel(x_ref, g_ref, w1_ref, w2_ref, gf_ref, o_ref, h_ref, *, final_norm):
    k = pl.program_id(1)

    @pl.when(k == 0)
    def _():
        xf = x_ref[...]
        h_ref[...] = _rmsnorm_rows(xf, g_ref[...]).astype(BF16)
        o_ref[...] = xf

    a = jnp.maximum(jnp.dot(h_ref[...], w1_ref[...], preferred_element_type=F32), 0.0)
    a = (a * a).astype(BF16)
    o_ref[...] += jnp.dot(a, w2_ref[...], preferred_element_type=F32)

    if final_norm:
        @pl.when(k == pl.num_programs(1) - 1)
        def _():
            o_ref[...] = _rmsnorm_rows(o_ref[...], gf_ref[...])


def _mlp(x, g, w1, w2, gf, l, *, tm, fk, final_norm):
    r, d = x.shape
    f = w1.shape[2]
    blocks = 2 * tm * d * 4 + 2 * d * fk * 2
    return pl.pallas_call(
        functools.partial(_mlp_kernel, final_norm=final_norm),
        grid=(r // tm, f // fk),
        in_specs=[pl.BlockSpec((tm, d), lambda i, k: (i, 0)),
                  _layer_spec(g, l),
                  pl.BlockSpec((None, d, fk), lambda i, k: (l, 0, k)),
                  pl.BlockSpec((None, fk, d), lambda i, k: (l, k, 0)),
                  pl.BlockSpec((1, d), lambda i, k: (0, 0))],
        out_specs=pl.BlockSpec((tm, d), lambda i, k: (i, 0)),
        out_shape=jax.ShapeDtypeStruct((r, d), F32),
        scratch_shapes=[pltpu.VMEM((tm, d), BF16)],
        compiler_params=_params(("parallel", "arbitrary"),
                                _vmem_limit(blocks, tm * d * 2, 2 * tm * fk * 4 + tm * d * 4)),
        name="mlp",
    )(x, g, w1, w2, gf)


def _tiles(r, nb, ktaps):
    tm_seq = min(r, 256)
    while tm_seq < (ktaps - 1) * nb:
        tm_seq *= 2
    tm_mm = min(r, 1024)
    assert r % tm_seq == 0 and r % tm_mm == 0 and tm_seq % nb == 0
    return tm_mm, tm_seq


def kernel(x, norm_mix, w_in, w_dw, b_dw, ln_g, ln_b, w_conv_out, a_re, a_im, log_dt, b_re, b_im,
           c_re, c_im, d_skip, w_glu, w_ssm_out, w_out, norm_mlp, w_ff1, w_ff2, norm_final):
    bsz, seq, d = x.shape
    depth = w_in.shape[0]
    ktaps, cw = w_dw.shape[1:]
    sw = w_glu.shape[1]
    assert bsz % V7X_SUBLANES == 0 and cw == sw and d == 2 * cw
    r = seq * bsz
    tm_mm, tm_seq = _tiles(r, bsz, ktaps)
    gate_col = 2 * cw + sw

    lbr, lbi, bbr, bbi = _ssm_prep(a_re, a_im, log_dt, b_re, b_im)
    wb, wc, lamr, lami = _ssm_layouts(lbr, lbi, bbr, bbi, c_re, c_im, bsz)
    rows = lambda v: v[:, None, :]
    conv_p = (w_dw, rows(b_dw), rows(ln_g), rows(ln_b), w_conv_out.astype(BF16))
    ssm_p = (wb, wc, lamr, lami, rows(d_skip), w_glu.astype(BF16), w_ssm_out.astype(BF16))
    w_in_b, w_out_b = w_in.astype(BF16), w_out.astype(BF16)
    w_ff1_b, w_ff2_b = w_ff1.astype(BF16), w_ff2.astype(BF16)
    g_mix, g_mlp, g_fin = rows(norm_mix), rows(norm_mlp), norm_final.reshape(1, d)

    xt = x.transpose(1, 0, 2).reshape(r, d)
    for l in range(depth):
        proj = _inproj(xt, g_mix, w_in_b, l, gate_col=gate_col, tm=tm_mm, tn=cw)
        xt = _mixer(proj, xt, conv_p, ssm_p, w_out_b, l, tm=tm_seq, nb=bsz)
        xt = _mlp(xt, g_mlp, w_ff1_b, w_ff2_b, g_fin, l, tm=tm_mm, fk=512, final_norm=(l == depth - 1))
    return xt.reshape(seq, bsz, d).transpose(1, 0, 2)
```

```python
import functools

import jax
import jax.numpy as jnp
from jax import lax
from jax.experimental import pallas as pl
from jax.experimental.pallas import tpu as pltpu

RMS_EPS = 1e-6
LN_EPS = 1e-5

F32 = jnp.float32
BF16 = jnp.bfloat16

V7X_VMEM_BYTES = 64 * 1024 * 1024
V7X_LANES = 128
V7X_SUBLANES = 8
SSM_CHUNK = V7X_LANES
CONV_TILES = 8


def _vmem_limit(block_bytes, single_bytes, temp_bytes):
    need = 2 * block_bytes + single_bytes + temp_bytes
    return int(min(V7X_VMEM_BYTES - 6 * 1024 * 1024, max(need, 16 * 1024 * 1024)))


def _params(semantics, limit):
    return pltpu.CompilerParams(dimension_semantics=semantics, vmem_limit_bytes=limit)


def _layer_spec(a, l, single=False):
    mode = dict(pipeline_mode=pl.Buffered(1)) if single else {}
    return pl.BlockSpec((None,) + a.shape[1:], lambda *_: (l,) + (0,) * (a.ndim - 1), **mode)


def _rmsnorm_rows(xf, g):
    ms = jnp.mean(xf * xf, axis=-1, keepdims=True)
    return xf * lax.rsqrt(ms + RMS_EPS) * g


def _ssm_prep_kernel(are_ref, aim_ref, ldt_ref, bre_ref, bim_ref,
                     lbr_ref, lbi_ref, bbr_ref, bbi_ref):
    ar = are_ref[...]
    ai = aim_ref[...]
    dt = jnp.exp(ldt_ref[...])
    mag = jnp.exp(ar * dt)
    lbr = mag * jnp.cos(ai * dt)
    lbi = mag * jnp.sin(ai * dt)
    lbr_ref[...] = lbr
    lbi_ref[...] = lbi
    nr = lbr - 1.0
    den = ar * ar + ai * ai
    fr = (nr * ar + lbi * ai) / den
    fi = (lbi * ar - nr * ai) / den
    br = bre_ref[...]
    bi = bim_ref[...]
    bbr_ref[...] = fr * br - fi * bi
    bbi_ref[...] = fr * bi + fi * br


def _ssm_prep(a_re, a_im, log_dt, b_re, b_im):
    nl, g, p = a_re.shape
    h = b_re.shape[-1]
    gp = g * p
    are = a_re.reshape(nl, 1, gp)
    aim = a_im.reshape(nl, 1, gp)
    ldt = jnp.broadcast_to(log_dt[:, :, None], (nl, g, p)).reshape(nl, 1, gp)
    bre = b_re.reshape(nl, gp, h).transpose(0, 2, 1)
    bim = b_im.reshape(nl, gp, h).transpose(0, 2, 1)
    row = pl.BlockSpec((None, 1, gp), lambda l: (l, 0, 0))
    mat = pl.BlockSpec((None, h, gp), lambda l: (l, 0, 0))
    return pl.pallas_call(
        _ssm_prep_kernel,
        grid=(nl,),
        in_specs=[row, row, row, mat, mat],
        out_specs=[row, row, mat, mat],
        out_shape=[jax.ShapeDtypeStruct((nl, 1, gp), F32)] * 2
        + [jax.ShapeDtypeStruct((nl, h, gp), F32)] * 2,
        name="ssm_prep",
    )(are, aim, ldt, bre, bim)


def _block_diag(w):
    nl, nc, gl, a, b = w.shape
    eye = jnp.eye(gl, dtype=w.dtype)
    return jnp.einsum("lcgab,gh->lcgahb", w, eye).reshape(nl, nc, gl * a, gl * b)


def _ssm_layouts(lbr, lbi, bbr, bbi, c_re, c_im, nb):
    nl, h, gp = bbr.shape
    g = c_re.shape[1]
    p = gp // g
    gl = SSM_CHUNK // h
    nc = g // gl
    def in_map(bb):
        bb = bb.reshape(nl, h, nc, gl, p).transpose(0, 2, 3, 1, 4)
        return _block_diag(bb)
    wb = jnp.concatenate([in_map(bbr), in_map(bbi)], axis=-1).astype(BF16)
    def out_map(c):
        c = c.reshape(nl, nc, gl, h, p).transpose(0, 1, 2, 4, 3)
        return _block_diag(c)
    wc = jnp.concatenate([out_map(c_re), out_map(-c_im)], axis=-2).astype(BF16)
    def lam_map(lb):
        return jnp.broadcast_to(lb.reshape(nl, nc, 1, gl * p), (nl, nc, nb, gl * p))
    return wb, wc, lam_map(lbr), lam_map(lbi)


def _inproj_kernel(x_ref, g_ref, w_ref, o_ref, h_ref, *, gate_col):
    j = pl.program_id(1)

    @pl.when(j == 0)
    def _():
        h_ref[...] = _rmsnorm_rows(x_ref[...], g_ref[...]).astype(BF16)

    acc = jnp.dot(h_ref[...], w_ref[...], preferred_element_type=F32)
    col = j * acc.shape[1] + lax.broadcasted_iota(jnp.int32, (1, acc.shape[1]), 1)
    o_ref[...] = jnp.where(col >= gate_col, jax.nn.sigmoid(acc), acc)


def _inproj(x, g, w, l, *, gate_col, tm, tn):
    r, d = x.shape
    n = w.shape[2]
    blocks = tm * d * 4 + d * tn * 2 + tm * tn * 4
    return pl.pallas_call(
        functools.partial(_inproj_kernel, gate_col=gate_col),
        grid=(r // tm, n // tn),
        in_specs=[pl.BlockSpec((tm, d), lambda i, j: (i, 0)),
                  _layer_spec(g, l),
                  pl.BlockSpec((None, d, tn), lambda i, j: (l, 0, j))],
        out_specs=pl.BlockSpec((tm, tn), lambda i, j: (i, j)),
        out_shape=jax.ShapeDtypeStruct((r, n), F32),
        scratch_shapes=[pltpu.VMEM((tm, d), BF16)],
        compiler_params=_params(("parallel", "arbitrary"),
                                _vmem_limit(blocks, tm * d * 2, tm * d * 4 + 2 * tm * tn * 4)),
        name="in_proj",
    )(x, g, w)


def _interleave(streams, totals):
    done = [0.0] * len(streams)
    live = [True] * len(streams)
    while any(live):
        k = min((s for s in range(len(streams)) if live[s]), key=lambda s: done[s] / totals[s])
        try:
            done[k] += next(streams[k])
        except StopIteration:
            live[k] = False


_COST = dict(glu=100, conv=130, ln=400, bu=256, scan=256, cy=350, sglu=300, sso=256, cvo=300, out=512)


def _cur_stream(val_ref, gate_ref, u_ref, wdw_ref, bdw_ref, lng_ref, lnb_ref, wb_ref, wc_ref, lamr_ref,
                lami_ref, dskip_ref, ubuf, cbuf, st_ref, y_w, y16_w, s16_w, *, tm, nb, ktaps):
    cw = val_ref.shape[1]
    halo = (ktaps - 1) * nb
    rb = CONV_TILES * nb
    half = st_ref.shape[2] // 2
    n_chunks = u_ref.shape[1] // SSM_CHUNK
    blocks = [(b, c) for b in range(tm // rb) for c in range(cw // V7X_LANES)]
    per_chunk = -(-len(blocks) // n_chunks)
    glu_done = set()

    def glu_piece(b):
        rows = slice(b * rb, (b + 1) * rb)
        ubuf[halo + b * rb:halo + (b + 1) * rb, :] = val_ref[rows, :] * jax.nn.sigmoid(gate_ref[rows, :])

    def conv_block(b, c):
        cs = slice(c * V7X_LANES, (c + 1) * V7X_LANES)
        r0 = pl.multiple_of(jnp.minimum(pl.program_id(0), 0) * rb + b * rb, rb)
        tiles = {}

        def tile(m):
            if m not in tiles:
                tiles[m] = ubuf[pl.ds(r0 + m * nb, nb), cs]
            return tiles[m]

        acc = [jnp.broadcast_to(bdw_ref[:, cs], (nb, V7X_LANES))] * CONV_TILES
        for k in range(ktaps):
            wk = jnp.broadcast_to(wdw_ref[k:k + 1, cs], (nb, V7X_LANES))
            for j in range(CONV_TILES):
                acc[j] = acc[j] + wk * tile(j + k)
        for j in range(CONV_TILES):
            cbuf[pl.ds(r0 + j * nb, nb), cs] = acc[j]

    def norm_piece(b):
        rows = slice(b * rb, (b + 1) * rb)
        c = cbuf[rows, :]
        mu = jnp.mean(c, axis=-1, keepdims=True)
        xc = c - mu
        var = jnp.mean(xc * xc, axis=-1, keepdims=True)
        v = xc * lax.rsqrt(var + LN_EPS) * lng_ref[...] + lnb_ref[...]
        s16_w[rows, :] = (v * jax.nn.sigmoid(v)).astype(BF16)

    def conv_items(blks):
        for b, c in blks:
            if b not in glu_done:
                glu_done.add(b)
                glu_piece(b)
                yield _COST["glu"]
            conv_block(b, c)
            yield _COST["conv"]
            if c == cw // V7X_LANES - 1:
                norm_piece(b)
                yield _COST["ln"]

    for c in range(n_chunks):
        mine = blocks[c * per_chunk:(c + 1) * per_chunk]
        cs = slice(c * SSM_CHUNK, (c + 1) * SSM_CHUNK)
        uc = u_ref[:, cs]
        bu = jnp.dot(uc.astype(BF16), wb_ref[c], preferred_element_type=F32)
        yield _COST["bu"]
        yield from conv_items(mine[:len(mine) // 2])
        lr = lamr_ref[c]
        li = lami_ref[c]
        sr = st_ref[c, :, 0:half]
        si = st_ref[c, :, half:2 * half]
        srs, sis = [], []
        for t in range(tm // nb):
            rows = slice(t * nb, (t + 1) * nb)
            sr, si = (lr * sr - li * si + bu[rows, 0:half],
                      lr * si + li * sr + bu[rows, half:2 * half])
            srs.append(sr)
            sis.append(si)
        st_ref[c, :, 0:half] = sr
        st_ref[c, :, half:2 * half] = si
        states = jnp.concatenate([jnp.concatenate(srs, axis=0), jnp.concatenate(sis, axis=0)], axis=1)
        yield _COST["scan"]
        yield from conv_items(mine[len(mine) // 2:])
        yc = jnp.dot(states.astype(BF16), wc_ref[c], preferred_element_type=F32)
        y = jax.nn.gelu(yc + dskip_ref[:, cs] * uc)
        y_w[:, cs] = y
        y16_w[:, cs] = y.astype(BF16)
        yield _COST["cy"]
    ubuf[0:halo, :] = ubuf[tm:tm + halo, :]


def _prev_stream(gc_refs, gs_refs, x_ref, wcout_ref, wglu_ref, wsout_ref, wout_ref,
                 o_ref, y_r, y16_r, s16_r, *, piece_cols):
    sw = wglu_ref.shape[0]
    d = wout_ref.shape[1]
    col = lambda n: slice(n * piece_cols, (n + 1) * piece_cols)
    y16 = y16_r[...]
    s16 = s16_r[...]

    zs = []
    for n in range(sw // piece_cols):
        t = jnp.dot(y16, wglu_ref[:, col(n)], preferred_element_type=F32)
        zs.append((y_r[:, col(n)] * jax.nn.sigmoid(t)).astype(BF16))
        yield _COST["sglu"]
    z16 = jnp.concatenate(zs, axis=1)

    def gate(refs, n):
        per = refs[0].shape[1] // piece_cols
        return refs[n // per][:, col(n % per)]

    ms = []
    for n in range(d // piece_cols):
        y_ssm = jnp.dot(z16, wsout_ref[:, col(n)], preferred_element_type=F32)
        yield _COST["sso"]
        y_conv = jnp.dot(s16, wcout_ref[:, col(n)], preferred_element_type=F32)
        ms.append((gate(gc_refs, n) * y_conv + gate(gs_refs, n) * y_ssm).astype(BF16))
        yield _COST["cvo"]
    m16 = jnp.concatenate(ms, axis=1)

    for n in range(d // piece_cols):
        o_ref[:, col(n)] = x_ref[:, col(n)] + jnp.dot(m16, wout_ref[:, col(n)], preferred_element_type=F32)
        yield _COST["out"]


def _mixer_kernel(val_ref, gate_ref, u_ref, gc0_ref, gc1_ref, gs0_ref, gs1_ref, x_ref,
                  wdw_ref, bdw_ref, lng_ref, lnb_ref, wcout_ref,
                  wb_ref, wc_ref, lamr_ref, lami_ref, dskip_ref, wglu_ref, wsout_ref, wout_ref,
                  o_ref, ubuf, cbuf, st_ref, ybuf, y16buf, s16buf, *, tm, nb, ktaps):
    i = pl.program_id(0)
    cw = val_ref.shape[1]
    slot = lax.rem(i, 2)

    @pl.when(i == 0)
    def _():
        ubuf[0:(ktaps - 1) * nb, :] = jnp.zeros(((ktaps - 1) * nb, cw), F32)
        st_ref[...] = jnp.zeros(st_ref.shape, F32)
        ybuf[1] = jnp.zeros(ybuf.shape[1:], F32)
        y16buf[1] = jnp.zeros(y16buf.shape[1:], BF16)
        s16buf[1] = jnp.zeros(s16buf.shape[1:], BF16)

    cur = _cur_stream(val_ref, gate_ref, u_ref, wdw_ref, bdw_ref, lng_ref, lnb_ref, wb_ref, wc_ref,
                      lamr_ref, lami_ref, dskip_ref, ubuf, cbuf, st_ref,
                      ybuf.at[slot], y16buf.at[slot], s16buf.at[slot], tm=tm, nb=nb, ktaps=ktaps)
    prev = _prev_stream((gc0_ref, gc1_ref), (gs0_ref, gs1_ref), x_ref, wcout_ref, wglu_ref, wsout_ref,
                        wout_ref, o_ref, ybuf.at[1 - slot], y16buf.at[1 - slot], s16buf.at[1 - slot],
                        piece_cols=2 * V7X_LANES)
    n_blocks = (tm // (CONV_TILES * nb)) * (cw // V7X_LANES)
    n_chunks = u_ref.shape[1] // SSM_CHUNK
    n_cols = wout_ref.shape[1] // (2 * V7X_LANES)
    total_cur = (tm // (CONV_TILES * nb)) * (_COST["glu"] + _COST["ln"]) + n_blocks * _COST["conv"] + (
        n_chunks * (_COST["bu"] + _COST["scan"] + _COST["cy"]))
    total_prev = (n_cols // 2) * _COST["sglu"] + n_cols * (_COST["sso"] + _COST["cvo"] + _COST["out"])
    _interleave([cur, prev], [total_cur, total_prev])


def _mixer(proj, x, conv_p, ssm_p, w_out, l, *, tm, nb):
    r, d = x.shape
    w_dw, b_dw, ln_g, ln_b, w_cout = conv_p
    wb, wc, lamr, lami, d_skip, w_glu, w_sout = ssm_p
    ktaps, cw = w_dw.shape[1:]
    sw = w_glu.shape[1]
    nc, _, lanes = wb.shape[1:]
    halo = (ktaps - 1) * nb
    n = r // tm
    assert cw == sw and d == 2 * cw and tm >= halo and tm % (CONV_TILES * nb) == 0
    cur = lambda j: pl.BlockSpec((tm, cw), lambda i: (jnp.minimum(i, n - 1), j))
    prev = lambda j: pl.BlockSpec((tm, cw), lambda i: (jnp.maximum(i - 1, 0), j))
    prev_row = pl.BlockSpec((tm, d), lambda i: (jnp.maximum(i - 1, 0), 0))
    params = (w_dw, b_dw, ln_g, ln_b, w_cout, wb, wc, lamr, lami, d_skip, w_glu, w_sout, w_out)
    single = sum(a[0].size * a.dtype.itemsize for a in params)
    scratch = (halo + tm) * cw * 4 + tm * cw * 4 + nc * nb * lanes * 4 + 2 * tm * (sw * 6 + cw * 2)
    blocks = 7 * tm * cw * 4 + 2 * tm * d * 4
    return pl.pallas_call(
        functools.partial(_mixer_kernel, tm=tm, nb=nb, ktaps=ktaps),
        grid=(n + 1,),
        in_specs=[cur(0), cur(1), cur(2), prev(3), prev(4), prev(5), prev(6), prev_row]
        + [_layer_spec(a, l, single=True) for a in params],
        out_specs=prev_row,
        out_shape=jax.ShapeDtypeStruct((r, d), F32),
        scratch_shapes=[pltpu.VMEM((halo + tm, cw), F32), pltpu.VMEM((tm, cw), F32),
                        pltpu.VMEM((nc, nb, lanes), F32), pltpu.VMEM((2, tm, sw), F32),
                        pltpu.VMEM((2, tm, sw), BF16), pltpu.VMEM((2, tm, cw), BF16)],
        compiler_params=_params(("arbitrary",), _vmem_limit(blocks, single + scratch, 10 * tm * cw * 4)),
        name="mixer",
    )(*([proj] * 7), x, *params)


def _mlp_kernel(x_hbm, g_ref, w1_ref, w2_ref, gf_ref, o_ref, h_ref, xbuf, xsem, *, tm, final_norm):
    i = pl.program_id(0)
    k = pl.program_id(1)

    def x_copy(tile):
        rows = pl.ds(pl.multiple_of(tile * tm, tm), tm)
        return pltpu.make_async_copy(x_hbm.at[rows, :], xbuf, xsem)

    @pl.when((i == 0) & (k == 0))
    def _():
        x_copy(0).start()

    @pl.when(k == 0)
    def _():
        x_copy(i).wait()
        xf = xbuf[...]
        h_ref[...] = _rmsnorm_rows(xf, g_ref[...]).astype(BF16)
        o_ref[...] = xf

    @pl.when((k == 1) & (i + 1 < pl.num_programs(0)))
    def _():
        x_copy(i + 1).start()

    a = jnp.maximum(jnp.dot(h_ref[...], w1_ref[...], preferred_element_type=F32), 0.0)
    a = (a * a).astype(BF16)
    o_ref[...] += jnp.dot(a, w2_ref[...], preferred_element_type=F32)

    if final_norm:
        @pl.when(k == pl.num_programs(1) - 1)
        def _():
            o_ref[...] = _rmsnorm_rows(o_ref[...], gf_ref[...])


def _mlp(x, g, w1, w2, gf, l, *, tm, fk, final_norm):
    r, d = x.shape
    f = w1.shape[2]
    assert f // fk >= 2
    blocks = tm * d * 4 + 2 * d * fk * 2
    return pl.pallas_call(
        functools.partial(_mlp_kernel, tm=tm, final_norm=final_norm),
        grid=(r // tm, f // fk),
        in_specs=[pl.BlockSpec(memory_space=pl.ANY),
                  _layer_spec(g, l),
                  pl.BlockSpec((None, d, fk), lambda i, k: (l, 0, k)),
                  pl.BlockSpec((None, fk, d), lambda i, k: (l, k, 0)),
                  pl.BlockSpec((1, d), lambda i, k: (0, 0))],
        out_specs=pl.BlockSpec((tm, d), lambda i, k: (i, 0)),
        out_shape=jax.ShapeDtypeStruct((r, d), F32),
        scratch_shapes=[pltpu.VMEM((tm, d), BF16), pltpu.VMEM((tm, d), F32), pltpu.SemaphoreType.DMA(())],
        compiler_params=_params(("arbitrary", "arbitrary"),
                                _vmem_limit(blocks, tm * d * 6, 2 * tm * fk * 4 + tm * d * 4)),
        name="mlp",
    )(x, g, w1, w2, gf)


def _tiles(r, nb, ktaps, in_cols, d_ff):
    tm_seq = min(r, 256)
    while tm_seq < (ktaps - 1) * nb:
        tm_seq *= 2
    tm_mm = min(r, 1024)
    assert r % tm_seq == 0 and r % tm_mm == 0 and tm_seq % nb == 0
    mxu_cols = 2 * V7X_LANES
    tn = in_cols // 4 if in_cols % (4 * mxu_cols) == 0 else mxu_cols
    fk = 1024 if d_ff % 2048 == 0 else d_ff // 2
    return tm_mm, tm_seq, tn, fk


def kernel(x, norm_mix, w_in, w_dw, b_dw, ln_g, ln_b, w_conv_out, a_re, a_im, log_dt, b_re, b_im,
           c_re, c_im, d_skip, w_glu, w_ssm_out, w_out, norm_mlp, w_ff1, w_ff2, norm_final):
    bsz, seq, d = x.shape
    depth = w_in.shape[0]
    ktaps, cw = w_dw.shape[1:]
    sw = w_glu.shape[1]
    assert bsz % V7X_SUBLANES == 0 and cw == sw and d == 2 * cw
    r = seq * bsz
    tm_mm, tm_seq, tn, fk = _tiles(r, bsz, ktaps, w_in.shape[2], w_ff1.shape[2])
    gate_col = 2 * cw + sw

    lbr, lbi, bbr, bbi = _ssm_prep(a_re, a_im, log_dt, b_re, b_im)
    wb, wc, lamr, lami = _ssm_layouts(lbr, lbi, bbr, bbi, c_re, c_im, bsz)
    rows = lambda v: v[:, None, :]
    conv_p = (w_dw, rows(b_dw), rows(ln_g), rows(ln_b), w_conv_out.astype(BF16))
    ssm_p = (wb, wc, lamr, lami, rows(d_skip), w_glu.astype(BF16), w_ssm_out.astype(BF16))
    w_in_b, w_out_b = w_in.astype(BF16), w_out.astype(BF16)
    w_ff1_b, w_ff2_b = w_ff1.astype(BF16), w_ff2.astype(BF16)
    g_mix, g_mlp, g_fin = rows(norm_mix), rows(norm_mlp), norm_final.reshape(1, d)

    xt = x.transpose(1, 0, 2).reshape(r, d)
    for l in range(depth):
        proj = _inproj(xt, g_mix, w_in_b, l, gate_col=gate_col, tm=tm_mm, tn=tn)
        xt = _mixer(proj, xt, conv_p, ssm_p, w_out_b, l, tm=tm_seq, nb=bsz)
        xt = _mlp(xt, g_mlp, w_ff1_b, w_ff2_b, g_fin, l, tm=tm_mm, fk=fk, final_norm=(l == depth - 1))
    return xt.reshape(seq, bsz, d).transpose(1, 0, 2)
```

```python
import functools

import jax
import jax.numpy as jnp
from jax import lax
from jax.experimental import pallas as pl
from jax.experimental.pallas import tpu as pltpu

RMS_EPS = 1e-6
LN_EPS = 1e-5

F32 = jnp.float32
BF16 = jnp.bfloat16

V7X_VMEM_BYTES = 64 * 1024 * 1024
V7X_LANES = 128
V7X_SUBLANES = 8
SSM_CHUNK = V7X_LANES
CONV_TILES = 8


def _vmem_limit(block_bytes, single_bytes, temp_bytes):
    need = 2 * block_bytes + single_bytes + temp_bytes
    return int(min(V7X_VMEM_BYTES - 6 * 1024 * 1024, max(need, 16 * 1024 * 1024)))


def _params(semantics, limit):
    return pltpu.CompilerParams(dimension_semantics=semantics, vmem_limit_bytes=limit)


def _layer_spec(a, l, single=False):
    mode = dict(pipeline_mode=pl.Buffered(1)) if single else {}
    return pl.BlockSpec((None,) + a.shape[1:], lambda *_: (l,) + (0,) * (a.ndim - 1), **mode)


def _rmsnorm_rows(xf, g):
    ms = jnp.mean(xf * xf, axis=-1, keepdims=True)
    return xf * lax.rsqrt(ms + RMS_EPS) * g


def _ssm_prep_kernel(are_ref, aim_ref, ldt_ref, bre_ref, bim_ref,
                     lbr_ref, lbi_ref, bbr_ref, bbi_ref):
    ar = are_ref[...]
    ai = aim_ref[...]
    dt = jnp.exp(ldt_ref[...])
    mag = jnp.exp(ar * dt)
    lbr = mag * jnp.cos(ai * dt)
    lbi = mag * jnp.sin(ai * dt)
    lbr_ref[...] = lbr
    lbi_ref[...] = lbi
    nr = lbr - 1.0
    den = ar * ar + ai * ai
    fr = (nr * ar + lbi * ai) / den
    fi = (lbi * ar - nr * ai) / den
    br = bre_ref[...]
    bi = bim_ref[...]
    bbr_ref[...] = fr * br - fi * bi
    bbi_ref[...] = fr * bi + fi * br


def _ssm_prep(a_re, a_im, log_dt, b_re, b_im):
    nl, g, p = a_re.shape
    h = b_re.shape[-1]
    gp = g * p
    are = a_re.reshape(nl, 1, gp)
    aim = a_im.reshape(nl, 1, gp)
    ldt = jnp.broadcast_to(log_dt[:, :, None], (nl, g, p)).reshape(nl, 1, gp)
    bre = b_re.reshape(nl, gp, h).transpose(0, 2, 1)
    bim = b_im.reshape(nl, gp, h).transpose(0, 2, 1)
    row = pl.BlockSpec((None, 1, gp), lambda l: (l, 0, 0))
    mat = pl.BlockSpec((None, h, gp), lambda l: (l, 0, 0))
    return pl.pallas_call(
        _ssm_prep_kernel,
        grid=(nl,),
        in_specs=[row, row, row, mat, mat],
        out_specs=[row, row, mat, mat],
        out_shape=[jax.ShapeDtypeStruct((nl, 1, gp), F32)] * 2
        + [jax.ShapeDtypeStruct((nl, h, gp), F32)] * 2,
        name="ssm_prep",
    )(are, aim, ldt, bre, bim)


def _block_diag(w):
    nl, nc, gl, a, b = w.shape
    eye = jnp.eye(gl, dtype=w.dtype)
    return jnp.einsum("lcgab,gh->lcgahb", w, eye).reshape(nl, nc, gl * a, gl * b)


def _ssm_layouts(lbr, lbi, bbr, bbi, c_re, c_im, nb):
    nl, h, gp = bbr.shape
    g = c_re.shape[1]
    p = gp // g
    gl = SSM_CHUNK // h
    nc = g // gl
    cat = jnp.concatenate

    def in_map(bb):
        bb = bb.reshape(nl, h, nc, gl, p).transpose(0, 2, 3, 1, 4)
        return _block_diag(bb)

    def out_map(c):
        c = c.reshape(nl, nc, gl, h, p).transpose(0, 1, 2, 4, 3)
        return _block_diag(c)

    def skip_map(k):
        k = k.reshape(nl, nc, gl, h, h).transpose(0, 1, 2, 4, 3)
        return _block_diag(k)

    def lam_map(lb):
        return jnp.broadcast_to(lb.reshape(nl, nc, 1, gl * p), (nl, nc, nb, gl * p))

    lbb_r = lbr * bbr - lbi * bbi
    lbb_i = lbr * bbi + lbi * bbr
    wb = cat([cat([in_map(lbb_r), in_map(lbb_i)], -1), cat([in_map(bbr), in_map(bbi)], -1)], -2)

    lr = lbr.reshape(nl, g, 1, p)
    li = lbi.reshape(nl, g, 1, p)
    l2r, l2i = lr * lr - li * li, 2.0 * lr * li
    cl_r, cl_i = c_re * lr - c_im * li, c_re * li + c_im * lr
    cl2_r, cl2_i = c_re * l2r - c_im * l2i, c_re * l2i + c_im * l2r
    b_r = bbr.reshape(nl, h, g, p).transpose(0, 2, 3, 1)
    b_i = bbi.reshape(nl, h, g, p).transpose(0, 2, 3, 1)
    mm = lambda a, b: jnp.einsum("lghp,lgpk->lghk", a, b, precision=lax.Precision.HIGHEST)
    k0 = skip_map(mm(c_re, b_r) - mm(c_im, b_i))
    k1 = skip_map(mm(cl_r, b_r) - mm(cl_i, b_i))
    wc = cat([cat([out_map(cl_r), out_map(cl2_r)], -1),
              cat([out_map(-cl_i), out_map(-cl2_i)], -1),
              cat([k0, k1], -1),
              cat([jnp.zeros_like(k0), k0], -1)], -2)
    lam2r = lbr * lbr - lbi * lbi
    lam2i = 2.0 * lbr * lbi
    return wb.astype(BF16), wc.astype(BF16), lam_map(lam2r), lam_map(lam2i)


def _inproj_kernel(x_ref, g_ref, w_ref, o_ref, h_ref, *, gate_col):
    j = pl.program_id(1)

    @pl.when(j == 0)
    def _():
        h_ref[...] = _rmsnorm_rows(x_ref[...], g_ref[...]).astype(BF16)

    acc = jnp.dot(h_ref[...], w_ref[...], preferred_element_type=F32)
    col = j * acc.shape[1] + lax.broadcasted_iota(jnp.int32, (1, acc.shape[1]), 1)
    o_ref[...] = jnp.where(col >= gate_col, jax.nn.sigmoid(acc), acc)


def _inproj(x, g, w, l, *, gate_col, tm, tn):
    r, d = x.shape
    n = w.shape[2]
    blocks = tm * d * 4 + d * tn * 2 + tm * tn * 4
    return pl.pallas_call(
        functools.partial(_inproj_kernel, gate_col=gate_col),
        grid=(r // tm, n // tn),
        in_specs=[pl.BlockSpec((tm, d), lambda i, j: (i, 0)),
                  _layer_spec(g, l),
                  pl.BlockSpec((None, d, tn), lambda i, j: (l, 0, j))],
        out_specs=pl.BlockSpec((tm, tn), lambda i, j: (i, j)),
        out_shape=jax.ShapeDtypeStruct((r, n), F32),
        scratch_shapes=[pltpu.VMEM((tm, d), BF16)],
        compiler_params=_params(("parallel", "arbitrary"),
                                _vmem_limit(blocks, tm * d * 2, tm * d * 4 + 2 * tm * tn * 4)),
        name="in_proj",
    )(x, g, w)


def _interleave(streams, totals):
    done = [0.0] * len(streams)
    live = [True] * len(streams)
    while any(live):
        k = min((s for s in range(len(streams)) if live[s]), key=lambda s: done[s] / totals[s])
        try:
            done[k] += next(streams[k])
        except StopIteration:
            live[k] = False


_COST = dict(glu=100, conv=130, ln=400, bu=256, scan=256, cy=350, sglu=300, sso=256, cvo=300, out=512)


def _cur_stream(val_ref, gate_ref, u_ref, wdw_ref, bdw_ref, lng_ref, lnb_ref, wb_ref, wc_ref, lamr_ref,
                lami_ref, dskip_ref, ubuf, cbuf, st_ref, y_w, y16_w, s16_w, *, tm, nb, ktaps):
    cw = val_ref.shape[1]
    halo = (ktaps - 1) * nb
    rb = CONV_TILES * nb
    half = st_ref.shape[2] // 2
    pairs = tm // (2 * nb)
    n_chunks = u_ref.shape[1] // SSM_CHUNK
    blocks = [(b, c) for b in range(tm // rb) for c in range(cw // V7X_LANES)]
    per_chunk = -(-len(blocks) // n_chunks)
    glu_done = set()

    def glu_piece(b):
        rows = slice(b * rb, (b + 1) * rb)
        ubuf[halo + b * rb:halo + (b + 1) * rb, :] = val_ref[rows, :] * jax.nn.sigmoid(gate_ref[rows, :])

    def conv_block(b, c):
        cs = slice(c * V7X_LANES, (c + 1) * V7X_LANES)
        r0 = pl.multiple_of(jnp.minimum(pl.program_id(0), 0) * rb + b * rb, rb)
        tiles = {}

        def tile(m):
            if m not in tiles:
                tiles[m] = ubuf[pl.ds(r0 + m * nb, nb), cs]
            return tiles[m]

        acc = [jnp.broadcast_to(bdw_ref[:, cs], (nb, V7X_LANES))] * CONV_TILES
        for k in range(ktaps):
            wk = jnp.broadcast_to(wdw_ref[k:k + 1, cs], (nb, V7X_LANES))
            for j in range(CONV_TILES):
                acc[j] = acc[j] + wk * tile(j + k)
        for j in range(CONV_TILES):
            cbuf[pl.ds(r0 + j * nb, nb), cs] = acc[j]

    def norm_piece(b):
        rows = slice(b * rb, (b + 1) * rb)
        c = cbuf[rows, :]
        mu = jnp.mean(c, axis=-1, keepdims=True)
        xc = c - mu
        var = jnp.mean(xc * xc, axis=-1, keepdims=True)
        v = xc * lax.rsqrt(var + LN_EPS) * lng_ref[...] + lnb_ref[...]
        s16_w[rows, :] = (v * jax.nn.sigmoid(v)).astype(BF16)

    def conv_items(blks):
        for b, c in blks:
            if b not in glu_done:
                glu_done.add(b)
                glu_piece(b)
                yield _COST["glu"]
            conv_block(b, c)
            yield _COST["conv"]
            if c == cw // V7X_LANES - 1:
                norm_piece(b)
                yield _COST["ln"]

    for c in range(n_chunks):
        mine = blocks[c * per_chunk:(c + 1) * per_chunk]
        cs = slice(c * SSM_CHUNK, (c + 1) * SSM_CHUNK)
        uc = u_ref[:, cs]
        ue = jnp.concatenate([uc[2 * q * nb:(2 * q + 1) * nb] for q in range(pairs)], axis=0)
        uo = jnp.concatenate([uc[(2 * q + 1) * nb:(2 * q + 2) * nb] for q in range(pairs)], axis=0)
        u2 = jnp.concatenate([ue, uo], axis=1).astype(BF16)
        bu = jnp.dot(u2, wb_ref[c], preferred_element_type=F32)
        yield _COST["bu"]
        yield from conv_items(mine[:len(mine) // 2])
        lr = lamr_ref[c]
        li = lami_ref[c]
        sr = st_ref[c, :, 0:half]
        si = st_ref[c, :, half:2 * half]
        srs, sis = [], []
        for q in range(pairs):
            rows = slice(q * nb, (q + 1) * nb)
            srs.append(sr)
            sis.append(si)
            sr, si = (lr * sr - li * si + bu[rows, 0:half],
                      lr * si + li * sr + bu[rows, half:2 * half])
        st_ref[c, :, 0:half] = sr
        st_ref[c, :, half:2 * half] = si
        before = jnp.concatenate([jnp.concatenate(srs, axis=0), jnp.concatenate(sis, axis=0)], axis=1)
        yield _COST["scan"]
        yield from conv_items(mine[len(mine) // 2:])
        lhs = jnp.concatenate([before.astype(BF16), u2], axis=1)
        y2 = jnp.dot(lhs, wc_ref[c], preferred_element_type=F32)
        dsk = dskip_ref[:, cs]
        ye = jax.nn.gelu(y2[:, 0:SSM_CHUNK] + dsk * ue)
        yo = jax.nn.gelu(y2[:, SSM_CHUNK:2 * SSM_CHUNK] + dsk * uo)
        y = jnp.concatenate([v[q * nb:(q + 1) * nb] for q in range(pairs) for v in (ye, yo)], axis=0)
        y_w[:, cs] = y
        y16_w[:, cs] = y.astype(BF16)
        yield _COST["cy"]
    ubuf[0:halo, :] = ubuf[tm:tm + halo, :]


def _prev_stream(gc_refs, gs_refs, wcout_ref, wglu_ref, wsout_ref, wout_ref,
                 o_ref, y_r, y16_r, s16_r, *, piece_cols):
    sw = wglu_ref.shape[0]
    d = wout_ref.shape[1]
    col = lambda n: slice(n * piece_cols, (n + 1) * piece_cols)
    y16 = y16_r[...]
    s16 = s16_r[...]

    zs = []
    for n in range(sw // piece_cols):
        t = jnp.dot(y16, wglu_ref[:, col(n)], preferred_element_type=F32)
        zs.append((y_r[:, col(n)] * jax.nn.sigmoid(t)).astype(BF16))
        yield _COST["sglu"]
    z16 = jnp.concatenate(zs, axis=1)

    def gate(refs, n):
        per = refs[0].shape[1] // piece_cols
        return refs[n // per][:, col(n % per)]

    ms = []
    for n in range(d // piece_cols):
        y_ssm = jnp.dot(z16, wsout_ref[:, col(n)], preferred_element_type=F32)
        yield _COST["sso"]
        y_conv = jnp.dot(s16, wcout_ref[:, col(n)], preferred_element_type=F32)
        ms.append((gate(gc_refs, n) * y_conv + gate(gs_refs, n) * y_ssm).astype(BF16))
        yield _COST["cvo"]
    m16 = jnp.concatenate(ms, axis=1)

    for n in range(d // piece_cols):
        o_ref[:, col(n)] = jnp.dot(m16, wout_ref[:, col(n)], preferred_element_type=F32)
        yield _COST["out"]


def _mixer_kernel(val_ref, gate_ref, u_ref, gc0_ref, gc1_ref, gs0_ref, gs1_ref,
                  wdw_ref, bdw_ref, lng_ref, lnb_ref, wcout_ref,
                  wb_ref, wc_ref, lamr_ref, lami_ref, dskip_ref, wglu_ref, wsout_ref, wout_ref,
                  o_ref, ubuf, cbuf, st_ref, ybuf, y16buf, s16buf, *, tm, nb, ktaps):
    i = pl.program_id(0)
    cw = val_ref.shape[1]
    slot = lax.rem(i, 2)

    @pl.when(i == 0)
    def _():
        ubuf[0:(ktaps - 1) * nb, :] = jnp.zeros(((ktaps - 1) * nb, cw), F32)
        st_ref[...] = jnp.zeros(st_ref.shape, F32)
        ybuf[1] = jnp.zeros(ybuf.shape[1:], F32)
        y16buf[1] = jnp.zeros(y16buf.shape[1:], BF16)
        s16buf[1] = jnp.zeros(s16buf.shape[1:], BF16)

    cur = _cur_stream(val_ref, gate_ref, u_ref, wdw_ref, bdw_ref, lng_ref, lnb_ref, wb_ref, wc_ref,
                      lamr_ref, lami_ref, dskip_ref, ubuf, cbuf, st_ref,
                      ybuf.at[slot], y16buf.at[slot], s16buf.at[slot], tm=tm, nb=nb, ktaps=ktaps)
    prev = _prev_stream((gc0_ref, gc1_ref), (gs0_ref, gs1_ref), wcout_ref, wglu_ref, wsout_ref,
                        wout_ref, o_ref, ybuf.at[1 - slot], y16buf.at[1 - slot], s16buf.at[1 - slot],
                        piece_cols=2 * V7X_LANES)
    n_blocks = (tm // (CONV_TILES * nb)) * (cw // V7X_LANES)
    n_chunks = u_ref.shape[1] // SSM_CHUNK
    n_cols = wout_ref.shape[1] // (2 * V7X_LANES)
    total_cur = (tm // (CONV_TILES * nb)) * (_COST["glu"] + _COST["ln"]) + n_blocks * _COST["conv"] + (
        n_chunks * (_COST["bu"] + _COST["scan"] + _COST["cy"]))
    total_prev = (n_cols // 2) * _COST["sglu"] + n_cols * (_COST["sso"] + _COST["cvo"] + _COST["out"])
    _interleave([cur, prev], [total_cur, total_prev])


def _mixer(proj, conv_p, ssm_p, w_out, l, *, tm, nb):
    r = proj.shape[0]
    d = w_out.shape[2]
    w_dw, b_dw, ln_g, ln_b, w_cout = conv_p
    wb, wc, lamr, lami, d_skip, w_glu, w_sout = ssm_p
    ktaps, cw = w_dw.shape[1:]
    sw = w_glu.shape[1]
    nc, _, lanes = wb.shape[1:]
    halo = (ktaps - 1) * nb
    n = r // tm
    assert cw == sw and d == 2 * cw and tm >= halo and tm % (CONV_TILES * nb) == 0
    cur = lambda j: pl.BlockSpec((tm, cw), lambda i: (jnp.minimum(i, n - 1), j))
    prev = lambda j: pl.BlockSpec((tm, cw), lambda i: (jnp.maximum(i - 1, 0), j))
    prev_row = pl.BlockSpec((tm, d), lambda i: (jnp.maximum(i - 1, 0), 0))
    params = (w_dw, b_dw, ln_g, ln_b, w_cout, wb, wc, lamr, lami, d_skip, w_glu, w_sout, w_out)
    single = sum(a[0].size * a.dtype.itemsize for a in params)
    scratch = (halo + tm) * cw * 4 + tm * cw * 4 + nc * nb * lanes * 4 + 2 * tm * (sw * 6 + cw * 2)
    blocks = 7 * tm * cw * 4 + tm * d * 4
    return pl.pallas_call(
        functools.partial(_mixer_kernel, tm=tm, nb=nb, ktaps=ktaps),
        grid=(n + 1,),
        in_specs=[cur(0), cur(1), cur(2), prev(3), prev(4), prev(5), prev(6)]
        + [_layer_spec(a, l, single=True) for a in params],
        out_specs=prev_row,
        out_shape=jax.ShapeDtypeStruct((r, d), F32),
        scratch_shapes=[pltpu.VMEM((halo + tm, cw), F32), pltpu.VMEM((tm, cw), F32),
                        pltpu.VMEM((nc, nb, lanes), F32), pltpu.VMEM((2, tm, sw), F32),
                        pltpu.VMEM((2, tm, sw), BF16), pltpu.VMEM((2, tm, cw), BF16)],
        compiler_params=_params(("arbitrary",), _vmem_limit(blocks, single + scratch, 10 * tm * cw * 4)),
        name="mixer",
    )(*([proj] * 7), *params)


def _mlp_kernel(x_hbm, mix_hbm, g_ref, w1_ref, w2_ref, gf_ref, o_ref, h_ref, xbuf, mbuf, sems,
                *, tm, final_norm):
    i = pl.program_id(0)
    k = pl.program_id(1)

    def copies(tile):
        rows = pl.ds(pl.multiple_of(tile * tm, tm), tm)
        return (pltpu.make_async_copy(x_hbm.at[rows, :], xbuf, sems.at[0]),
                pltpu.make_async_copy(mix_hbm.at[rows, :], mbuf, sems.at[1]))

    @pl.when((i == 0) & (k == 0))
    def _():
        for cp in copies(0):
            cp.start()

    @pl.when(k == 0)
    def _():
        for cp in copies(i):
            cp.wait()
        step = min(tm, 16 * V7X_SUBLANES)
        for r0 in range(0, tm, step):
            rows = slice(r0, r0 + step)
            xf = xbuf[rows, :] + mbuf[rows, :]
            o_ref[rows, :] = xf
            h_ref[rows, :] = _rmsnorm_rows(xf, g_ref[...]).astype(BF16)

    @pl.when((k == 1) & (i + 1 < pl.num_programs(0)))
    def _():
        for cp in copies(i + 1):
            cp.start()

    a = jnp.maximum(jnp.dot(h_ref[...], w1_ref[...], preferred_element_type=F32), 0.0)
    a = (a * a).astype(BF16)
    o_ref[...] += jnp.dot(a, w2_ref[...], preferred_element_type=F32)

    if final_norm:
        @pl.when(k == pl.num_programs(1) - 1)
        def _():
            o_ref[...] = _rmsnorm_rows(o_ref[...], gf_ref[...])


def _mlp(x, mix, g, w1, w2, gf, l, *, tm, fk, final_norm):
    r, d = x.shape
    f = w1.shape[2]
    assert f // fk >= 2
    blocks = tm * d * 4 + 2 * d * fk * 2
    return pl.pallas_call(
        functools.partial(_mlp_kernel, tm=tm, final_norm=final_norm),
        grid=(r // tm, f // fk),
        in_specs=[pl.BlockSpec(memory_space=pl.ANY),
                  pl.BlockSpec(memory_space=pl.ANY),
                  _layer_spec(g, l),
                  pl.BlockSpec((None, d, fk), lambda i, k: (l, 0, k)),
                  pl.BlockSpec((None, fk, d), lambda i, k: (l, k, 0)),
                  pl.BlockSpec((1, d), lambda i, k: (0, 0))],
        out_specs=pl.BlockSpec((tm, d), lambda i, k: (i, 0)),
        out_shape=jax.ShapeDtypeStruct((r, d), F32),
        scratch_shapes=[pltpu.VMEM((tm, d), BF16), pltpu.VMEM((tm, d), F32), pltpu.VMEM((tm, d), F32),
                        pltpu.SemaphoreType.DMA((2,))],
        compiler_params=_params(("arbitrary", "arbitrary"),
                                _vmem_limit(blocks, tm * d * 10, 2 * tm * fk * 4 + tm * d * 4)),
        name="mlp",
    )(x, mix, g, w1, w2, gf)


def _tiles(r, nb, ktaps, in_cols, d_ff):
    tm_seq = min(r, 256)
    while tm_seq < (ktaps - 1) * nb:
        tm_seq *= 2
    tm_mm = min(r, 1024)
    assert r % tm_seq == 0 and r % tm_mm == 0 and tm_seq % nb == 0
    mxu_cols = 2 * V7X_LANES
    tn = in_cols // 4 if in_cols % (4 * mxu_cols) == 0 else mxu_cols
    fk = 1024 if d_ff % 2048 == 0 else d_ff // 2
    return tm_mm, tm_seq, tn, fk


def kernel(x, norm_mix, w_in, w_dw, b_dw, ln_g, ln_b, w_conv_out, a_re, a_im, log_dt, b_re, b_im,
           c_re, c_im, d_skip, w_glu, w_ssm_out, w_out, norm_mlp, w_ff1, w_ff2, norm_final):
    bsz, seq, d = x.shape
    depth = w_in.shape[0]
    ktaps, cw = w_dw.shape[1:]
    sw = w_glu.shape[1]
    assert bsz % V7X_SUBLANES == 0 and cw == sw and d == 2 * cw
    r = seq * bsz
    tm_mm, tm_seq, tn, fk = _tiles(r, bsz, ktaps, w_in.shape[2], w_ff1.shape[2])
    gate_col = 2 * cw + sw

    lbr, lbi, bbr, bbi = _ssm_prep(a_re, a_im, log_dt, b_re, b_im)
    wb, wc, lamr, lami = _ssm_layouts(lbr, lbi, bbr, bbi, c_re, c_im, bsz)
    rows = lambda v: v[:, None, :]
    conv_p = (w_dw, rows(b_dw), rows(ln_g), rows(ln_b), w_conv_out.astype(BF16))
    ssm_p = (wb, wc, lamr, lami, rows(d_skip), w_glu.astype(BF16), w_ssm_out.astype(BF16))
    w_in_b, w_out_b = w_in.astype(BF16), w_out.astype(BF16)
    w_ff1_b, w_ff2_b = w_ff1.astype(BF16), w_ff2.astype(BF16)
    g_mix, g_mlp, g_fin = rows(norm_mix), rows(norm_mlp), norm_final.reshape(1, d)

    xt = x.transpose(1, 0, 2).reshape(r, d)
    for l in range(depth):
        proj = _inproj(xt, g_mix, w_in_b, l, gate_col=gate_col, tm=tm_mm, tn=tn)
        mix = _mixer(proj, conv_p, ssm_p, w_out_b, l, tm=tm_seq, nb=bsz)
        xt = _mlp(xt, mix, g_mlp, w_ff1_b, w_ff2_b, g_fin, l, tm=tm_mm, fk=fk, final_norm=(l == depth - 1))
    return xt.reshape(seq, bsz, d).transpose(1, 0, 2)
```

```python
import functools

import jax
import jax.numpy as jnp
from jax import lax
from jax.experimental import pallas as pl
from jax.experimental.pallas import tpu as pltpu

RMS_EPS = 1e-6
LN_EPS = 1e-5

F32 = jnp.float32
BF16 = jnp.bfloat16

V7X_VMEM_BYTES = 64 * 1024 * 1024
V7X_LANES = 128
V7X_SUBLANES = 8
SSM_CHUNK = V7X_LANES
CONV_TILES = 8


def _vmem_limit(block_bytes, single_bytes, temp_bytes):
    need = 2 * block_bytes + single_bytes + temp_bytes
    return int(min(V7X_VMEM_BYTES - 6 * 1024 * 1024, max(need, 16 * 1024 * 1024)))


def _params(semantics, limit):
    return pltpu.CompilerParams(dimension_semantics=semantics, vmem_limit_bytes=limit)


def _layer_spec(a, l, single=False):
    mode = dict(pipeline_mode=pl.Buffered(1)) if single else {}
    return pl.BlockSpec((None,) + a.shape[1:], lambda *_: (l,) + (0,) * (a.ndim - 1), **mode)


def _rmsnorm_rows(xf, g):
    ms = jnp.mean(xf * xf, axis=-1, keepdims=True)
    return xf * lax.rsqrt(ms + RMS_EPS) * g


def _ssm_prep_kernel(are_ref, aim_ref, ldt_ref, bre_ref, bim_ref,
                     lbr_ref, lbi_ref, bbr_ref, bbi_ref):
    ar = are_ref[...]
    ai = aim_ref[...]
    dt = jnp.exp(ldt_ref[...])
    mag = jnp.exp(ar * dt)
    lbr = mag * jnp.cos(ai * dt)
    lbi = mag * jnp.sin(ai * dt)
    lbr_ref[...] = lbr
    lbi_ref[...] = lbi
    nr = lbr - 1.0
    den = ar * ar + ai * ai
    fr = (nr * ar + lbi * ai) / den
    fi = (lbi * ar - nr * ai) / den
    br = bre_ref[...]
    bi = bim_ref[...]
    bbr_ref[...] = fr * br - fi * bi
    bbi_ref[...] = fr * bi + fi * br


def _ssm_prep(a_re, a_im, log_dt, b_re, b_im):
    nl, g, p = a_re.shape
    h = b_re.shape[-1]
    gp = g * p
    are = a_re.reshape(nl, 1, gp)
    aim = a_im.reshape(nl, 1, gp)
    ldt = jnp.broadcast_to(log_dt[:, :, None], (nl, g, p)).reshape(nl, 1, gp)
    bre = b_re.reshape(nl, gp, h).transpose(0, 2, 1)
    bim = b_im.reshape(nl, gp, h).transpose(0, 2, 1)
    row = pl.BlockSpec((None, 1, gp), lambda l: (l, 0, 0))
    mat = pl.BlockSpec((None, h, gp), lambda l: (l, 0, 0))
    return pl.pallas_call(
        _ssm_prep_kernel,
        grid=(nl,),
        in_specs=[row, row, row, mat, mat],
        out_specs=[row, row, mat, mat],
        out_shape=[jax.ShapeDtypeStruct((nl, 1, gp), F32)] * 2
        + [jax.ShapeDtypeStruct((nl, h, gp), F32)] * 2,
        name="ssm_prep",
    )(are, aim, ldt, bre, bim)


def _block_diag(w):
    nl, nc, gl, a, b = w.shape
    eye = jnp.eye(gl, dtype=w.dtype)
    return jnp.einsum("lcgab,gh->lcgahb", w, eye).reshape(nl, nc, gl * a, gl * b)


def _ssm_layouts(lbr, lbi, bbr, bbi, c_re, c_im, nb):
    nl, h, gp = bbr.shape
    g = c_re.shape[1]
    p = gp // g
    gl = SSM_CHUNK // h
    nc = g // gl
    cat = jnp.concatenate

    def in_map(bb):
        bb = bb.reshape(nl, h, nc, gl, p).transpose(0, 2, 3, 1, 4)
        return _block_diag(bb)

    def out_map(c):
        c = c.reshape(nl, nc, gl, h, p).transpose(0, 1, 2, 4, 3)
        return _block_diag(c)

    def skip_map(k):
        k = k.reshape(nl, nc, gl, h, h).transpose(0, 1, 2, 4, 3)
        return _block_diag(k)

    def lam_map(lb):
        return jnp.broadcast_to(lb.reshape(nl, nc, 1, gl * p), (nl, nc, nb, gl * p))

    lbb_r = lbr * bbr - lbi * bbi
    lbb_i = lbr * bbi + lbi * bbr
    wb = cat([cat([in_map(lbb_r), in_map(lbb_i)], -1), cat([in_map(bbr), in_map(bbi)], -1)], -2)

    lr = lbr.reshape(nl, g, 1, p)
    li = lbi.reshape(nl, g, 1, p)
    l2r, l2i = lr * lr - li * li, 2.0 * lr * li
    cl_r, cl_i = c_re * lr - c_im * li, c_re * li + c_im * lr
    cl2_r, cl2_i = c_re * l2r - c_im * l2i, c_re * l2i + c_im * l2r
    b_r = bbr.reshape(nl, h, g, p).transpose(0, 2, 3, 1)
    b_i = bbi.reshape(nl, h, g, p).transpose(0, 2, 3, 1)
    mm = lambda a, b: jnp.einsum("lghp,lgpk->lghk", a, b, precision=lax.Precision.HIGHEST)
    k0 = skip_map(mm(c_re, b_r) - mm(c_im, b_i))
    k1 = skip_map(mm(cl_r, b_r) - mm(cl_i, b_i))
    wc = cat([cat([out_map(cl_r), out_map(cl2_r)], -1),
              cat([out_map(-cl_i), out_map(-cl2_i)], -1),
              cat([k0, k1], -1),
              cat([jnp.zeros_like(k0), k0], -1)], -2)
    lam2r = lbr * lbr - lbi * lbi
    lam2i = 2.0 * lbr * lbi
    return wb.astype(BF16), wc.astype(BF16), lam_map(lam2r), lam_map(lam2i)


def _inproj_kernel(x_ref, g_ref, w_ref, o_ref, h_ref):
    @pl.when(pl.program_id(1) == 0)
    def _():
        h_ref[...] = _rmsnorm_rows(x_ref[...], g_ref[...]).astype(BF16)

    o_ref[...] = jnp.dot(h_ref[...], w_ref[...], preferred_element_type=F32)


def _inproj(x, g, w, l, *, tm, tn):
    r, d = x.shape
    n = w.shape[2]
    blocks = tm * d * 4 + d * tn * 2 + tm * tn * 4
    return pl.pallas_call(
        _inproj_kernel,
        grid=(r // tm, n // tn),
        in_specs=[pl.BlockSpec((tm, d), lambda i, j: (i, 0)),
                  _layer_spec(g, l),
                  pl.BlockSpec((None, d, tn), lambda i, j: (l, 0, j))],
        out_specs=pl.BlockSpec((tm, tn), lambda i, j: (i, j)),
        out_shape=jax.ShapeDtypeStruct((r, n), F32),
        scratch_shapes=[pltpu.VMEM((tm, d), BF16)],
        compiler_params=_params(("parallel", "arbitrary"),
                                _vmem_limit(blocks, tm * d * 2, tm * d * 4 + 2 * tm * tn * 4)),
        name="in_proj",
    )(x, g, w)


def _interleave(streams, totals):
    done = [0.0] * len(streams)
    live = [True] * len(streams)
    while any(live):
        k = min((s for s in range(len(streams)) if live[s]), key=lambda s: done[s] / totals[s])
        try:
            done[k] += next(streams[k])
        except StopIteration:
            live[k] = False


_COST = dict(glu=100, conv=130, ln=400, bu=256, scan=256, cy=350, sglu=300, sso=256, cvo=300, out=512)


def _cur_stream(val_ref, gate_ref, u_ref, wdw_ref, bdw_ref, lng_ref, lnb_ref, wb_ref, wc_ref, lamr_ref,
                lami_ref, dskip_ref, ubuf, cbuf, st_ref, y_w, y16_w, s16_w, *, tm, nb, ktaps):
    cw = val_ref.shape[1]
    halo = (ktaps - 1) * nb
    rb = CONV_TILES * nb
    half = st_ref.shape[2] // 2
    pairs = tm // (2 * nb)
    n_chunks = u_ref.shape[1] // SSM_CHUNK
    blocks = [(b, c) for b in range(tm // rb) for c in range(cw // V7X_LANES)]
    per_chunk = -(-len(blocks) // n_chunks)
    glu_done = set()

    def glu_piece(b):
        rows = slice(b * rb, (b + 1) * rb)
        ubuf[halo + b * rb:halo + (b + 1) * rb, :] = val_ref[rows, :] * jax.nn.sigmoid(gate_ref[rows, :])

    def conv_block(b, c):
        cs = slice(c * V7X_LANES, (c + 1) * V7X_LANES)
        r0 = pl.multiple_of(jnp.minimum(pl.program_id(0), 0) * rb + b * rb, rb)
        tiles = {}

        def tile(m):
            if m not in tiles:
                tiles[m] = ubuf[pl.ds(r0 + m * nb, nb), cs]
            return tiles[m]

        acc = [jnp.broadcast_to(bdw_ref[:, cs], (nb, V7X_LANES))] * CONV_TILES
        for k in range(ktaps):
            wk = jnp.broadcast_to(wdw_ref[k:k + 1, cs], (nb, V7X_LANES))
            for j in range(CONV_TILES):
                acc[j] = acc[j] + wk * tile(j + k)
        for j in range(CONV_TILES):
            cbuf[pl.ds(r0 + j * nb, nb), cs] = acc[j]

    def norm_piece(b):
        rows = slice(b * rb, (b + 1) * rb)
        c = cbuf[rows, :]
        mu = jnp.mean(c, axis=-1, keepdims=True)
        xc = c - mu
        var = jnp.mean(xc * xc, axis=-1, keepdims=True)
        v = xc * lax.rsqrt(var + LN_EPS) * lng_ref[...] + lnb_ref[...]
        s16_w[rows, :] = (v * jax.nn.sigmoid(v)).astype(BF16)

    def conv_items(blks):
        for b, c in blks:
            if b not in glu_done:
                glu_done.add(b)
                glu_piece(b)
                yield _COST["glu"]
            conv_block(b, c)
            yield _COST["conv"]
            if c == cw // V7X_LANES - 1:
                norm_piece(b)
                yield _COST["ln"]

    for c in range(n_chunks):
        mine = blocks[c * per_chunk:(c + 1) * per_chunk]
        cs = slice(c * SSM_CHUNK, (c + 1) * SSM_CHUNK)
        uc = u_ref[:, cs]
        ue = jnp.concatenate([uc[2 * q * nb:(2 * q + 1) * nb] for q in range(pairs)], axis=0)
        uo = jnp.concatenate([uc[(2 * q + 1) * nb:(2 * q + 2) * nb] for q in range(pairs)], axis=0)
        u2 = jnp.concatenate([ue, uo], axis=1).astype(BF16)
        bu = jnp.dot(u2, wb_ref[c], preferred_element_type=F32)
        yield _COST["bu"]
        yield from conv_items(mine[:len(mine) // 2])
        lr = lamr_ref[c]
        li = lami_ref[c]
        sr = st_ref[c, :, 0:half]
        si = st_ref[c, :, half:2 * half]
        srs, sis = [], []
        for q in range(pairs):
            rows = slice(q * nb, (q + 1) * nb)
            srs.append(sr)
            sis.append(si)
            sr, si = (lr * sr - li * si + bu[rows, 0:half],
                      lr * si + li * sr + bu[rows, half:2 * half])
        st_ref[c, :, 0:half] = sr
        st_ref[c, :, half:2 * half] = si
        before = jnp.concatenate([jnp.concatenate(srs, axis=0), jnp.concatenate(sis, axis=0)], axis=1)
        yield _COST["scan"]
        yield from conv_items(mine[len(mine) // 2:])
        lhs = jnp.concatenate([before.astype(BF16), u2], axis=1)
        y2 = jnp.dot(lhs, wc_ref[c], preferred_element_type=F32)
        dsk = dskip_ref[:, cs]
        ye = jax.nn.gelu(y2[:, 0:SSM_CHUNK] + dsk * ue)
        yo = jax.nn.gelu(y2[:, SSM_CHUNK:2 * SSM_CHUNK] + dsk * uo)
        y = jnp.concatenate([v[q * nb:(q + 1) * nb] for q in range(pairs) for v in (ye, yo)], axis=0)
        y_w[:, cs] = y
        y16_w[:, cs] = y.astype(BF16)
        yield _COST["cy"]
    ubuf[0:halo, :] = ubuf[tm:tm + halo, :]


def _prev_stream(gc_refs, gs_refs, wcout_ref, wglu_ref, wsout_ref, wout_ref,
                 o_ref, y_r, y16_r, s16_r, *, piece_cols):
    sw = wglu_ref.shape[0]
    d = wout_ref.shape[1]
    col = lambda n: slice(n * piece_cols, (n + 1) * piece_cols)
    y16 = y16_r[...]
    s16 = s16_r[...]

    zs = []
    for n in range(sw // piece_cols):
        t = jnp.dot(y16, wglu_ref[:, col(n)], preferred_element_type=F32)
        zs.append((y_r[:, col(n)] * jax.nn.sigmoid(t)).astype(BF16))
        yield _COST["sglu"]
    z16 = jnp.concatenate(zs, axis=1)

    def gate(refs, n):
        per = refs[0].shape[1] // piece_cols
        return jax.nn.sigmoid(refs[n // per][:, col(n % per)])

    ms = []
    for n in range(d // piece_cols):
        y_ssm = jnp.dot(z16, wsout_ref[:, col(n)], preferred_element_type=F32)
        yield _COST["sso"]
        y_conv = jnp.dot(s16, wcout_ref[:, col(n)], preferred_element_type=F32)
        ms.append((gate(gc_refs, n) * y_conv + gate(gs_refs, n) * y_ssm).astype(BF16))
        yield _COST["cvo"]
    m16 = jnp.concatenate(ms, axis=1)

    for n in range(d // piece_cols):
        o_ref[:, col(n)] = jnp.dot(m16, wout_ref[:, col(n)], preferred_element_type=F32)
        yield _COST["out"]


def _mixer_kernel(val_ref, gate_ref, u_ref, gc0_ref, gc1_ref, gs0_ref, gs1_ref,
                  wdw_ref, bdw_ref, lng_ref, lnb_ref, wcout_ref,
                  wb_ref, wc_ref, lamr_ref, lami_ref, dskip_ref, wglu_ref, wsout_ref, wout_ref,
                  o_ref, ubuf, cbuf, st_ref, ybuf, y16buf, s16buf, *, tm, nb, ktaps):
    i = pl.program_id(0)
    cw = val_ref.shape[1]
    slot = lax.rem(i, 2)

    @pl.when(i == 0)
    def _():
        ubuf[0:(ktaps - 1) * nb, :] = jnp.zeros(((ktaps - 1) * nb, cw), F32)
        st_ref[...] = jnp.zeros(st_ref.shape, F32)
        ybuf[1] = jnp.zeros(ybuf.shape[1:], F32)
        y16buf[1] = jnp.zeros(y16buf.shape[1:], BF16)
        s16buf[1] = jnp.zeros(s16buf.shape[1:], BF16)

    cur = _cur_stream(val_ref, gate_ref, u_ref, wdw_ref, bdw_ref, lng_ref, lnb_ref, wb_ref, wc_ref,
                      lamr_ref, lami_ref, dskip_ref, ubuf, cbuf, st_ref,
                      ybuf.at[slot], y16buf.at[slot], s16buf.at[slot], tm=tm, nb=nb, ktaps=ktaps)
    prev = _prev_stream((gc0_ref, gc1_ref), (gs0_ref, gs1_ref), wcout_ref, wglu_ref, wsout_ref,
                        wout_ref, o_ref, ybuf.at[1 - slot], y16buf.at[1 - slot], s16buf.at[1 - slot],
                        piece_cols=2 * V7X_LANES)
    n_blocks = (tm // (CONV_TILES * nb)) * (cw // V7X_LANES)
    n_chunks = u_ref.shape[1] // SSM_CHUNK
    n_cols = wout_ref.shape[1] // (2 * V7X_LANES)
    total_cur = (tm // (CONV_TILES * nb)) * (_COST["glu"] + _COST["ln"]) + n_blocks * _COST["conv"] + (
        n_chunks * (_COST["bu"] + _COST["scan"] + _COST["cy"]))
    total_prev = (n_cols // 2) * _COST["sglu"] + n_cols * (_COST["sso"] + _COST["cvo"] + _COST["out"])
    _interleave([cur, prev], [total_cur, total_prev])


def _mixer(proj, conv_p, ssm_p, w_out, l, *, tm, nb):
    r = proj.shape[0]
    d = w_out.shape[2]
    w_dw, b_dw, ln_g, ln_b, w_cout = conv_p
    wb, wc, lamr, lami, d_skip, w_glu, w_sout = ssm_p
    ktaps, cw = w_dw.shape[1:]
    sw = w_glu.shape[1]
    nc, _, lanes = wb.shape[1:]
    halo = (ktaps - 1) * nb
    n = r // tm
    assert cw == sw and d == 2 * cw and tm >= halo and tm % (CONV_TILES * nb) == 0
    cur = lambda j: pl.BlockSpec((tm, cw), lambda i: (jnp.minimum(i, n - 1), j))
    prev = lambda j: pl.BlockSpec((tm, cw), lambda i: (jnp.maximum(i - 1, 0), j))
    prev_row = pl.BlockSpec((tm, d), lambda i: (jnp.maximum(i - 1, 0), 0))
    params = (w_dw, b_dw, ln_g, ln_b, w_cout, wb, wc, lamr, lami, d_skip, w_glu, w_sout, w_out)
    single = sum(a[0].size * a.dtype.itemsize for a in params)
    scratch = (halo + tm) * cw * 4 + tm * cw * 4 + nc * nb * lanes * 4 + 2 * tm * (sw * 6 + cw * 2)
    blocks = 7 * tm * cw * 4 + tm * d * 4
    return pl.pallas_call(
        functools.partial(_mixer_kernel, tm=tm, nb=nb, ktaps=ktaps),
        grid=(n + 1,),
        in_specs=[cur(0), cur(1), cur(2), prev(3), prev(4), prev(5), prev(6)]
        + [_layer_spec(a, l, single=True) for a in params],
        out_specs=prev_row,
        out_shape=jax.ShapeDtypeStruct((r, d), F32),
        scratch_shapes=[pltpu.VMEM((halo + tm, cw), F32), pltpu.VMEM((tm, cw), F32),
                        pltpu.VMEM((nc, nb, lanes), F32), pltpu.VMEM((2, tm, sw), F32),
                        pltpu.VMEM((2, tm, sw), BF16), pltpu.VMEM((2, tm, cw), BF16)],
        compiler_params=_params(("arbitrary",), _vmem_limit(blocks, single + scratch, 10 * tm * cw * 4)),
        name="mixer",
    )(*([proj] * 7), *params)


def _mlp_kernel(x_hbm, mix_hbm, g_ref, w1_ref, w2_ref, gf_ref, o_ref, h_ref, xbuf, mbuf, sems,
                *, tm, final_norm):
    i = pl.program_id(0)
    k = pl.program_id(1)

    def copies(tile):
        rows = pl.ds(pl.multiple_of(tile * tm, tm), tm)
        return (pltpu.make_async_copy(x_hbm.at[rows, :], xbuf, sems.at[0]),
                pltpu.make_async_copy(mix_hbm.at[rows, :], mbuf, sems.at[1]))

    @pl.when((i == 0) & (k == 0))
    def _():
        for cp in copies(0):
            cp.start()

    @pl.when(k == 0)
    def _():
        for cp in copies(i):
            cp.wait()
        step = min(tm, 16 * V7X_SUBLANES)
        for r0 in range(0, tm, step):
            rows = slice(r0, r0 + step)
            xf = xbuf[rows, :] + mbuf[rows, :]
            o_ref[rows, :] = xf
            h_ref[rows, :] = _rmsnorm_rows(xf, g_ref[...]).astype(BF16)

    @pl.when((k == 1) & (i + 1 < pl.num_programs(0)))
    def _():
        for cp in copies(i + 1):
            cp.start()

    a = jnp.maximum(jnp.dot(h_ref[...], w1_ref[...], preferred_element_type=F32), 0.0)
    a = (a * a).astype(BF16)
    o_ref[...] += jnp.dot(a, w2_ref[...], preferred_element_type=F32)

    if final_norm:
        @pl.when(k == pl.num_programs(1) - 1)
        def _():
            o_ref[...] = _rmsnorm_rows(o_ref[...], gf_ref[...])


def _mlp(x, mix, g, w1, w2, gf, l, *, tm, fk, final_norm):
    r, d = x.shape
    f = w1.shape[2]
    assert f // fk >= 2
    blocks = tm * d * 4 + 2 * d * fk * 2
    return pl.pallas_call(
        functools.partial(_mlp_kernel, tm=tm, final_norm=final_norm),
        grid=(r // tm, f // fk),
        in_specs=[pl.BlockSpec(memory_space=pl.ANY),
                  pl.BlockSpec(memory_space=pl.ANY),
                  _layer_spec(g, l),
                  pl.BlockSpec((None, d, fk), lambda i, k: (l, 0, k)),
                  pl.BlockSpec((None, fk, d), lambda i, k: (l, k, 0)),
                  pl.BlockSpec((1, d), lambda i, k: (0, 0))],
        out_specs=pl.BlockSpec((tm, d), lambda i, k: (i, 0)),
        out_shape=jax.ShapeDtypeStruct((r, d), F32),
        scratch_shapes=[pltpu.VMEM((tm, d), BF16), pltpu.VMEM((tm, d), F32), pltpu.VMEM((tm, d), F32),
                        pltpu.SemaphoreType.DMA((2,))],
        compiler_params=_params(("arbitrary", "arbitrary"),
                                _vmem_limit(blocks, tm * d * 10, 2 * tm * fk * 4 + tm * d * 4)),
        name="mlp",
    )(x, mix, g, w1, w2, gf)


def _tiles(r, nb, ktaps, in_cols, d_ff):
    tm_seq = min(r, 256)
    while tm_seq < (ktaps - 1) * nb:
        tm_seq *= 2
    tm_mm = min(r, 1024)
    assert r % tm_seq == 0 and r % tm_mm == 0 and tm_seq % nb == 0
    mxu_cols = 2 * V7X_LANES
    tn = in_cols // 4 if in_cols % (4 * mxu_cols) == 0 else mxu_cols
    fk = 1024 if d_ff % 2048 == 0 else d_ff // 2
    return tm_mm, tm_seq, tn, fk


def kernel(x, norm_mix, w_in, w_dw, b_dw, ln_g, ln_b, w_conv_out, a_re, a_im, log_dt, b_re, b_im,
           c_re, c_im, d_skip, w_glu, w_ssm_out, w_out, norm_mlp, w_ff1, w_ff2, norm_final):
    bsz, seq, d = x.shape
    depth = w_in.shape[0]
    ktaps, cw = w_dw.shape[1:]
    sw = w_glu.shape[1]
    assert bsz % V7X_SUBLANES == 0 and cw == sw and d == 2 * cw
    r = seq * bsz
    tm_mm, tm_seq, tn, fk = _tiles(r, bsz, ktaps, w_in.shape[2], w_ff1.shape[2])

    lbr, lbi, bbr, bbi = _ssm_prep(a_re, a_im, log_dt, b_re, b_im)
    wb, wc, lamr, lami = _ssm_layouts(lbr, lbi, bbr, bbi, c_re, c_im, bsz)
    rows = lambda v: v[:, None, :]
    conv_p = (w_dw, rows(b_dw), rows(ln_g), rows(ln_b), w_conv_out.astype(BF16))
    ssm_p = (wb, wc, lamr, lami, rows(d_skip), w_glu.astype(BF16), w_ssm_out.astype(BF16))
    w_in_b, w_out_b = w_in.astype(BF16), w_out.astype(BF16)
    w_ff1_b, w_ff2_b = w_ff1.astype(BF16), w_ff2.astype(BF16)
    g_mix, g_mlp, g_fin = rows(norm_mix), rows(norm_mlp), norm_final.reshape(1, d)

    xt = x.transpose(1, 0, 2).reshape(r, d)
    for l in range(depth):
        proj = _inproj(xt, g_mix, w_in_b, l, tm=tm_mm, tn=tn)
        mix = _mixer(proj, conv_p, ssm_p, w_out_b, l, tm=tm_seq, nb=bsz)
        xt = _mlp(xt, mix, g_mlp, w_ff1_b, w_ff2_b, g_fin, l, tm=tm_mm, fk=fk, final_norm=(l == depth - 1))
    return xt.reshape(seq, bsz, d).transpose(1, 0, 2)
```
